```python
import jax, jax.numpy as jnp
from jax import lax
import numpy as np


D_MODEL = 1024
BATCH = 1
SEQ = 16384
DEPTH = 4

N_EVEN = (DEPTH + 1) // 2
N_ODD = DEPTH // 2
N_VRES = max(N_EVEN - 1, 0)

D_MIX = D_MODEL
HEAD_A = 64
D_A = D_MIX // 2
H_A = D_A // HEAD_A
LORA_W = 64
LORA_A = 64
LORA_V = 32
LORA_G = 128
GN_EPS = 64e-5
SPLIT_A = (D_A, 2 * D_A, 3 * D_A, 3 * D_A + LORA_W, 3 * D_A + LORA_W + LORA_A)
D_IN_A = 3 * D_A + LORA_W + LORA_A + LORA_G
D_B = D_MIX - D_A
CONV_B = 3
D_IN_EVEN = D_IN_A + 3 * D_B
D_C = D_MIX // 2
H_C = 8
BLK_C = D_C // H_C
CONV_C = 4
LRU_C = 8.0
D_D = D_MIX - D_C
POOL_WINDOWS = (2, 4, 8, 16)
N_POOL = len(POOL_WINDOWS)
G_D = D_D // N_POOL
D_IN_ODD = 2 * D_C + D_D
D_FF = 2816
RMS_EPS = 1e-6

kernel_name = 'hybrid_rwkv7_shortconv_rglru_pool_macaron'


def rmsnorm(x, g):
    xf = x.astype(jnp.float32)
    y = xf * lax.rsqrt(jnp.mean(xf * xf, axis=-1, keepdims=True) + RMS_EPS)
    return (y * g.astype(jnp.float32)).astype(x.dtype)


def swiglu(h, w_gate, w_up, w_down):
    return (jax.nn.silu(h @ w_gate) * (h @ w_up)) @ w_down


def token_shift(z):
    return jnp.pad(z[:, :-1], ((0, 0), (1, 0), (0, 0)))


def causal_dwconv(z, w):
    K, C = w.shape
    return lax.conv_general_dilated(
        z, w[:, None, :].astype(z.dtype), window_strides=(1,), padding=[(K - 1, 0)],
        dimension_numbers=('NWC', 'WIO', 'NWC'), feature_group_count=C)


def rwkv7_scan(r, decay, k, v, kk, a):
    B, T, H, N = r.shape

    def step(S, inp):
        r_t, w_t, k_t, v_t, kk_t, a_t = inp
        sk = jnp.einsum('bhvk,bhk->bhv', S, kk_t)
        S = (S * w_t[:, :, None, :]
             - sk[..., None] * (kk_t * a_t)[:, :, None, :]
             + v_t[..., None] * k_t[:, :, None, :])
        return S, jnp.einsum('bhvk,bhk->bhv', S, r_t)

    xs = tuple(jnp.moveaxis(z, 1, 0) for z in (r, decay, k, v, kk, a))
    S0 = jnp.zeros((B, H, N, N), jnp.float32)
    _, y = lax.scan(step, S0, xs)
    return jnp.moveaxis(y, 0, 1)


def rwkv7_time_mix(p, mu, w0, w_up, a0, a_up, g_up, k_k, k_a, r_k, ln_w, ln_b, v_first, v_res):
    B, T, _ = p.shape
    p = p.astype(jnp.float32)
    p = p + (token_shift(p) - p) * mu
    r, k, v, xw, xa, xg = jnp.split(p, SPLIT_A, axis=-1)
    w = -jax.nn.softplus(-(w0 + jnp.tanh(xw) @ w_up)) - 0.5
    decay = jnp.exp(-jnp.exp(w))
    a = jax.nn.sigmoid(a0 + xa @ a_up)
    g = jax.nn.sigmoid(xg) @ g_up
    if v_res is not None:
        v0, v_dn, v_up = v_res
        v = v + (v_first - v) * jax.nn.sigmoid(v0 + (v @ v_dn) @ v_up)
    heads = lambda z: z.reshape(B, T, H_A, HEAD_A)
    kk = heads(k * k_k)
    kk = kk * lax.rsqrt(jnp.maximum(jnp.sum(kk * kk, axis=-1, keepdims=True), 1e-24))
    k = k * (1.0 + (a - 1.0) * k_a)
    rh, kh, vh = heads(r), heads(k), heads(v)
    y = rwkv7_scan(rh, heads(decay), kh, vh, kk, heads(a))
    mean = jnp.mean(y, axis=-1, keepdims=True)
    var = jnp.mean(jnp.square(y - mean), axis=-1, keepdims=True)
    y = (y - mean) * lax.rsqrt(var + GN_EPS) * ln_w.reshape(H_A, HEAD_A) + ln_b.reshape(H_A, HEAD_A)
    y = y + jnp.sum(rh * kh * r_k, axis=-1, keepdims=True) * vh
    return y.reshape(B, T, D_A) * g, v


def short_conv_mix(p, conv_w):
    b, c, h = jnp.split(p.astype(jnp.float32), 3, axis=-1)
    return b * causal_dwconv(c * h, conv_w)


def rglru_mix(p, conv_w, conv_b, w_a, b_a, w_x, b_x, lam):
    B, T, _ = p.shape
    p = p.astype(jnp.float32)
    gate, u = p[..., :D_C], p[..., D_C:]
    u = causal_dwconv(u, conv_w) + conv_b
    ub = u.reshape(B, T, H_C, BLK_C)
    r = jax.nn.sigmoid(jnp.einsum('bthi,hij->bthj', ub, w_a) + b_a)
    i = jax.nn.sigmoid(jnp.einsum('bthi,hij->bthj', ub, w_x) + b_x)
    log_a = -LRU_C * r * jax.nn.softplus(-lam)
    a = jnp.exp(log_a)
    mult = jnp.sqrt(-jnp.expm1(2.0 * log_a))
    mult = jnp.where((jnp.arange(T) == 0)[None, :, None, None], 1.0, mult)
    bx = mult * i * ub

    def combine(c1, c2):
        a1, b1 = c1
        a2, b2 = c2
        return a1 * a2, a2 * b1 + b2

    _, h = lax.associative_scan(combine, (a, bx), axis=1)
    return jax.nn.gelu(gate) * h.reshape(B, T, D_C)


def multiscale_pool_mix(p, w_grp, scale):
    B, T, _ = p.shape
    pg = p.astype(jnp.float32).reshape(B, T, N_POOL, G_D)
    cs = jnp.cumsum(pg, axis=1)
    n_avail = jnp.arange(1, T + 1, dtype=jnp.float32)
    pooled = []
    for gi, win in enumerate(POOL_WINDOWS):
        c = cs[:, :, gi]
        c_lag = jnp.pad(c, ((0, 0), (win, 0), (0, 0)))[:, :T]
        pooled.append((c - c_lag) / jnp.minimum(n_avail, float(win))[None, :, None])
    d = jnp.stack(pooled, axis=2) - pg
    y = jnp.einsum('btgi,gij->btgj', d, w_grp) * scale.reshape(N_POOL, G_D)
    return y.reshape(B, T, D_D)


def setup_inputs(seed: int = 0) -> dict:
    key = jax.random.key(seed)
    ks = iter(jax.random.split(key, 40))
    nrm = lambda shape, s: s * jax.random.normal(next(ks), shape, jnp.float32)
    unif = lambda shape, lo, hi: jax.random.uniform(next(ks), shape, jnp.float32, lo, hi)
    lam_s = unif((N_ODD, H_C, BLK_C), 0.9, 0.999) ** (1.0 / LRU_C)
    return {
        'x': nrm((BATCH, SEQ, D_MODEL), 1.0),
        'norm_g': 1.0 + nrm((DEPTH, 3, D_MODEL), 0.02),
        'ffn_wg': nrm((DEPTH, 2, D_MODEL, D_FF), D_MODEL ** -0.5),
        'ffn_wu': nrm((DEPTH, 2, D_MODEL, D_FF), D_MODEL ** -0.5),
        'ffn_wd': nrm((DEPTH, 2, D_FF, D_MODEL), D_FF ** -0.5),
        'even_w_in': nrm((N_EVEN, D_MODEL, D_IN_EVEN), D_MODEL ** -0.5),
        'even_w_out': nrm((N_EVEN, D_MIX, D_MODEL), D_MIX ** -0.5),
        'a_mu': unif((N_EVEN, D_IN_A), 0.0, 1.0),
        'a_w0': unif((N_EVEN, D_A), -6.5, -1.5),
        'a_w_up': nrm((N_EVEN, LORA_W, D_A), 0.1 * LORA_W ** -0.5),
        'a_a0': nrm((N_EVEN, D_A), 0.1),
        'a_a_up': nrm((N_EVEN, LORA_A, D_A), 0.1 * LORA_A ** -0.5),
        'a_g_up': nrm((N_EVEN, LORA_G, D_A), LORA_G ** -0.5),
        'a_k_k': 0.85 + nrm((N_EVEN, D_A), 0.05),
        'a_k_a': 1.0 + nrm((N_EVEN, D_A), 0.05),
        'a_r_k': nrm((N_EVEN, H_A, HEAD_A), 0.1),
        'a_ln_w': 1.0 + nrm((N_EVEN, D_A), 0.02),
        'a_ln_b': nrm((N_EVEN, D_A), 0.02),
        'a_v0': 1.0 + nrm((N_VRES, D_A), 0.1),
        'a_v_dn': nrm((N_VRES, D_A, LORA_V), D_A ** -0.5),
        'a_v_up': nrm((N_VRES, LORA_V, D_A), 0.1 * LORA_V ** -0.5),
        'b_conv_w': nrm((N_EVEN, CONV_B, D_B), CONV_B ** -0.5),
        'odd_w_in': nrm((N_ODD, D_MODEL, D_IN_ODD), D_MODEL ** -0.5),
        'odd_w_out': nrm((N_ODD, D_MIX, D_MODEL), D_MIX ** -0.5),
        'c_conv_w': nrm((N_ODD, CONV_C, D_C), CONV_C ** -0.5),
        'c_conv_b': nrm((N_ODD, D_C), 0.02),
        'c_wa': nrm((N_ODD, H_C, BLK_C, BLK_C), BLK_C ** -0.5),
        'c_ba': nrm((N_ODD, H_C, BLK_C), 0.02),
        'c_wx': nrm((N_ODD, H_C, BLK_C, BLK_C), BLK_C ** -0.5),
        'c_bx': nrm((N_ODD, H_C, BLK_C), 0.02),
        'c_lam': jnp.log(lam_s) - jnp.log1p(-lam_s),
        'd_w': nrm((N_ODD, N_POOL, G_D, G_D), G_D ** -0.5),
        'd_scale': 1.0 + nrm((N_ODD, D_D), 0.1),
        'final_g': 1.0 + nrm((D_MODEL,), 0.02),
    }


def reference(x, norm_g, ffn_wg, ffn_wu, ffn_wd, even_w_in, even_w_out, a_mu, a_w0, a_w_up,
              a_a0, a_a_up, a_g_up, a_k_k, a_k_a, a_r_k, a_ln_w, a_ln_b, a_v0, a_v_dn, a_v_up,
              b_conv_w, odd_w_in, odd_w_out, c_conv_w, c_conv_b, c_wa, c_ba, c_wx, c_bx, c_lam,
              d_w, d_scale, final_g):
    dt = x.dtype
    v_first = None
    for layer in range(DEPTH):
        h = rmsnorm(x, norm_g[layer, 0])
        x = x + 0.5 * swiglu(h, ffn_wg[layer, 0], ffn_wu[layer, 0], ffn_wd[layer, 0])
        h = rmsnorm(x, norm_g[layer, 1])
        if layer % 2 == 0:
            e = layer // 2
            proj = h @ even_w_in[e]
            v_res = None if e == 0 else (a_v0[e - 1], a_v_dn[e - 1], a_v_up[e - 1])
            ya, v = rwkv7_time_mix(proj[..., :D_IN_A], a_mu[e], a_w0[e], a_w_up[e], a_a0[e],
                                   a_a_up[e], a_g_up[e], a_k_k[e], a_k_a[e], a_r_k[e],
                                   a_ln_w[e], a_ln_b[e], v_first, v_res)
            if e == 0:
                v_first = v
            yb = short_conv_mix(proj[..., D_IN_A:], b_conv_w[e])
            mix = jnp.concatenate([ya, yb], axis=-1).astype(dt) @ even_w_out[e]
        else:
            o = layer // 2
            proj = h @ odd_w_in[o]
            yc = rglru_mix(proj[..., :2 * D_C], c_conv_w[o], c_conv_b[o], c_wa[o], c_ba[o],
                           c_wx[o], c_bx[o], c_lam[o])
            yd = multiscale_pool_mix(proj[..., 2 * D_C:], d_w[o], d_scale[o])
            mix = jnp.concatenate([yc, yd], axis=-1).astype(dt) @ odd_w_out[o]
        x = x + mix
        h = rmsnorm(x, norm_g[layer, 2])
        x = x + 0.5 * swiglu(h, ffn_wg[layer, 1], ffn_wu[layer, 1], ffn_wd[layer, 1])
    return rmsnorm(x, final_g)
```

```python
import functools

import jax
import jax.numpy as jnp
from jax import lax
from jax.experimental import pallas as pl
from jax.experimental.pallas import tpu as pltpu

F32 = jnp.float32
BF16 = jnp.bfloat16
HIGHEST = lax.Precision.HIGHEST

D_MODEL = 1024
DEPTH = 4
D_A = 512
HEAD_A = 64
LORA_V = 32
D_IN_A = 1792
D_B = 512
D_C = 512
H_C = 8
D_D = 512
POOL_WINDOWS = (2, 4, 8, 16)
G_D = 128
D_FF = 2816
GN_EPS = 64e-5
RMS_EPS = 1e-6
LRU_C = 8.0

LANES = 128
SUBLANES = 8
MXU_DIM = 256
VMEM_LIMIT_BYTES = 56 * 1024 * 1024

TM_FFN = 512
FF_CHUNK = MXU_DIM
TT = 256
CHUNK = 64
GROUP_W = MXU_DIM
N_GROUPS = D_A // GROUP_W
HALO = SUBLANES
POOL_HALO = 2 * SUBLANES


def _params(n_axes=1):
    return pltpu.CompilerParams(dimension_semantics=("arbitrary",) * n_axes,
                                vmem_limit_bytes=VMEM_LIMIT_BYTES)


def _resident(shape):
    zeros = (0,) * len(shape)
    return pl.BlockSpec(shape, lambda i: zeros, pipeline_mode=pl.Buffered(1))


def _rows(width, tile):
    return pl.BlockSpec((tile, width), lambda i: (i, 0))


def _rmsnorm(x, g):
    ms = jnp.mean(x * x, axis=-1, keepdims=True)
    return x * lax.rsqrt(ms + RMS_EPS) * g


def _sigmoid(x):
    return 1.0 / (1.0 + jnp.exp(-x))


def _softplus(z):
    return jnp.maximum(z, 0.0) + jnp.log1p(jnp.exp(-jnp.abs(z)))


def _bdot(x, w):
    return jnp.dot(x.astype(BF16), w, preferred_element_type=F32)


def _mm(x, y):
    return jnp.dot(x, y, precision=HIGHEST, preferred_element_type=F32)


def _mm_nt(x, y):
    return lax.dot_general(x, y, (((1,), (1,)), ((), ())), precision=HIGHEST, preferred_element_type=F32)


def _mm_tn(x, y):
    return lax.dot_general(x, y, (((0,), (0,)), ((), ())), precision=HIGHEST, preferred_element_type=F32)


def _ffn_kernel(x_ref, g_ref, wg_ref, wu_ref, wd_ref, fg_ref, o_ref, act_ref, *, final_norm):
    x = x_ref[...]
    h = _rmsnorm(x, g_ref[...]).astype(BF16)
    for j in range(D_FF // FF_CHUNK):
        cols = slice(j * FF_CHUNK, (j + 1) * FF_CHUNK)
        gate = jnp.dot(h, wg_ref[:, cols], preferred_element_type=F32)
        up = jnp.dot(h, wu_ref[:, cols], preferred_element_type=F32)
        act_ref[:, cols] = (gate * _sigmoid(gate) * up).astype(BF16)
    y = x + 0.5 * jnp.dot(act_ref[...], wd_ref[...], preferred_element_type=F32)
    if final_norm:
        y = _rmsnorm(y, fg_ref[...])
    o_ref[...] = y


def _ffn(x, g, wg, wu, wd, layer, k, final_g=None):
    T = x.shape[0]
    w_in_spec = pl.BlockSpec((None, None, D_MODEL, D_FF), lambda i: (layer, k, 0, 0), pipeline_mode=pl.Buffered(1))
    w_out_spec = pl.BlockSpec((None, None, D_FF, D_MODEL), lambda i: (layer, k, 0, 0), pipeline_mode=pl.Buffered(1))
    fg = g if final_g is None else final_g
    return pl.pallas_call(
        functools.partial(_ffn_kernel, final_norm=final_g is not None),
        grid=(T // TM_FFN,),
        in_specs=[_rows(D_MODEL, TM_FFN), _resident((1, D_MODEL)), w_in_spec, w_in_spec, w_out_spec,
                  _resident((1, D_MODEL))],
        out_specs=_rows(D_MODEL, TM_FFN),
        out_shape=jax.ShapeDtypeStruct((T, D_MODEL), F32),
        scratch_shapes=[pltpu.VMEM((TM_FFN, D_FF), BF16)],
        compiler_params=_params(),
        name="ffn",
    )(x, g, wg, wu, wd, fg)


def _proj_kernel(x_ref, g_ref, w_ref, pa_ref, pb_ref):
    h = _rmsnorm(x_ref[...], g_ref[...]).astype(BF16)
    pa_ref[...] = jnp.dot(h, w_ref[:, :D_IN_A], preferred_element_type=F32)
    pb_ref[...] = jnp.dot(h, w_ref[:, D_IN_A:], preferred_element_type=F32)


def _even_proj(x, g, w_in):
    T = x.shape[0]
    n_b = 3 * D_B
    return pl.pallas_call(
        _proj_kernel,
        grid=(T // TT,),
        in_specs=[_rows(D_MODEL, TT), _resident((1, D_MODEL)), _resident((D_MODEL, D_IN_A + n_b))],
        out_specs=[_rows(D_IN_A, TT), _rows(n_b, TT)],
        out_shape=[jax.ShapeDtypeStruct((T, D_IN_A), F32), jax.ShapeDtypeStruct((T, n_b), F32)],
        compiler_params=_params(),
        name="even_proj",
    )(x, g, w_in)


def _rwkv_chunks(r_s, wl_s, k_s, v_s, kn_s, a_s, y_s, st_ref):
    L = CHUNK
    row = lax.broadcasted_iota(jnp.int32, (L, GROUP_W), 0)
    col = lax.broadcasted_iota(jnp.int32, (L, GROUP_W), 1) & (HEAD_A - 1)
    strict = row > col
    incl = row >= col
    eye_ss = (row == col).astype(F32)
    r256 = lax.broadcasted_iota(jnp.int32, (GROUP_W, GROUP_W), 0)
    c256 = lax.broadcasted_iota(jnp.int32, (GROUP_W, GROUP_W), 1)
    same_head = (r256 // HEAD_A) == (c256 // HEAD_A)
    eye256 = r256 == c256
    tri = (lax.broadcasted_iota(jnp.int32, (L, L), 0) >= lax.broadcasted_iota(jnp.int32, (L, L), 1)).astype(F32)
    reps = GROUP_W // L

    def blockdiag(x):
        return jnp.where(same_head, jnp.concatenate([x] * reps, axis=0), 0.0)

    def per_head(x_ss, y):
        return _mm(x_ss, blockdiag(y))

    def chunk_body(c, carry):
        r0 = pl.multiple_of(c * L, L)
        rows = pl.ds(r0, L)
        for gi in range(N_GROUPS):
            cols = slice(gi * GROUP_W, (gi + 1) * GROUP_W)
            r = r_s[rows, cols]
            wl = wl_s[rows, cols]
            k = k_s[rows, cols]
            v = v_s[rows, cols]
            kn = kn_s[rows, cols]
            a = a_s[rows, cols]

            cum = _mm(tri, wl)
            cum_end = cum[L - 1:L, :]
            b = kn * a
            inv_p = jnp.exp(-cum)
            kt = kn * jnp.exp(cum - wl)
            rt = r * jnp.exp(cum)
            bt = b * inv_p
            kd = k * inv_p
            to_end = jnp.exp(cum_end - cum)
            b_end = b * to_end
            k_end = k * to_end
            p_end = jnp.exp(cum_end)

            lhs = jnp.concatenate([kt, rt], axis=0)
            s_b = _mm_nt(lhs, blockdiag(bt))
            s_k = _mm_nt(lhs, blockdiag(kd))
            a_ab = jnp.where(strict, s_b[:L], 0.0)
            a_ak = jnp.where(strict, s_k[:L], 0.0)
            a_rb = jnp.where(incl, s_b[L:], 0.0)
            a_rk = jnp.where(incl, s_k[L:], 0.0)

            pw = -a_ab
            t_inv = eye_ss + pw
            for _ in range(5):
                pw = per_head(pw, pw)
                t_inv = t_inv + per_head(t_inv, pw)

            w_mat = -per_head(t_inv, kt)
            u0 = -per_head(t_inv, per_head(a_ak, v))
            r_hat = rt + per_head(a_rb, w_mat)
            y0 = per_head(a_rb, u0) + per_head(a_rk, v)
            m_t = jnp.where(same_head, _mm_tn(b_end, w_mat), 0.0) + jnp.where(eye256, p_end, 0.0)
            c_t = jnp.where(same_head, _mm_tn(b_end, u0) + _mm_tn(k_end, v), 0.0)

            st = st_ref[gi]
            y_s[rows, cols] = _mm(r_hat, st) + y0
            st_ref[gi] = _mm(m_t, st) + c_t
        return carry

    lax.fori_loop(0, TT // L, chunk_body, 0)


def _rwkv_kernel(*refs, has_vres):
    (p_ref, mu_ref, w0_ref, wup_ref, a0_ref, aup_ref, gup_ref, kk_ref, ka_ref, rk_ref, lnw_ref, lnb_ref,
     seg_ref) = refs[:13]
    n_in = 13
    if has_vres:
        vf_ref, v0_ref, vdn_ref, vup_ref = refs[13:17]
        n_in = 17
        ya_ref = refs[n_in]
        scratch = refs[n_in + 1:]
    else:
        ya_ref, vout_ref = refs[n_in:n_in + 2]
        scratch = refs[n_in + 2:]
    ext_ref, r_s, wl_s, k_s, v_s, kn_s, a_s, g_s, y_s, st_ref = scratch

    @pl.when(pl.program_id(0) == 0)
    def _():
        ext_ref[0:HALO, :] = jnp.zeros((HALO, D_IN_A), F32)
        st_ref[...] = jnp.zeros_like(st_ref)

    p = p_ref[...]
    ext_ref[HALO:HALO + TT, :] = p
    prev = ext_ref[HALO - 1:HALO - 1 + TT, :]
    ext_ref[0:HALO, :] = p[TT - HALO:TT, :]
    p = p + (prev - p) * mu_ref[...]

    r = p[:, 0:D_A]
    k = p[:, D_A:2 * D_A]
    v = p[:, 2 * D_A:3 * D_A]
    x_wa = p[:, 3 * D_A:3 * D_A + LANES]
    xg = p[:, 3 * D_A + LANES:]

    w = -_softplus(-(w0_ref[...] + _bdot(jnp.tanh(x_wa), wup_ref[...]))) - 0.5
    a = _sigmoid(a0_ref[...] + _bdot(x_wa, aup_ref[...]))
    g_s[...] = _bdot(_sigmoid(xg), gup_ref[...])
    if has_vres:
        low = _bdot(v, vdn_ref[...])
        v = v + (vf_ref[...] - v) * _sigmoid(v0_ref[...] + _bdot(low, vup_ref[...]))
    else:
        vout_ref[...] = v
    kn = k * kk_ref[...]
    kn = kn * lax.rsqrt(jnp.maximum(_mm(kn * kn, seg_ref[...]), 1e-24))
    k = k * (1.0 + (a - 1.0) * ka_ref[...])

    r_s[...] = r
    wl_s[...] = -jnp.exp(w)
    k_s[...] = k
    v_s[...] = v
    kn_s[...] = kn
    a_s[...] = a

    _rwkv_chunks(r_s, wl_s, k_s, v_s, kn_s, a_s, y_s, st_ref)

    y = y_s[...]
    inv_n = 1.0 / HEAD_A
    mean = _mm(y, seg_ref[...]) * inv_n
    d = y - mean
    var = _mm(d * d, seg_ref[...]) * inv_n
    y = d * lax.rsqrt(var + GN_EPS) * lnw_ref[...] + lnb_ref[...]
    y = y + _mm(r_s[...] * k_s[...] * rk_ref[...], seg_ref[...]) * v_s[...]
    ya_ref[...] = y * g_s[...]


def _rwkv(pa, prm, v_first, vres):
    T = pa.shape[0]
    has_vres = vres is not None
    vec = _resident((1, D_A))
    lora = _resident((LANES, D_A))
    in_specs = [_rows(D_IN_A, TT), _resident((1, D_IN_A)), vec, lora, vec, lora, lora, vec, vec, vec, vec, vec,
                _resident((D_A, D_A))]
    args = [pa, prm["mu"], prm["w0"], prm["w_up"], prm["a0"], prm["a_up"], prm["g_up"], prm["k_k"], prm["k_a"],
            prm["r_k"], prm["ln_w"], prm["ln_b"], prm["seg"]]
    out_specs = [_rows(D_A, TT)]
    out_shape = [jax.ShapeDtypeStruct((T, D_A), F32)]
    if has_vres:
        in_specs += [_rows(D_A, TT), vec, _resident((D_A, LANES)), lora]
        args += [v_first, vres["v0"], vres["v_dn"], vres["v_up"]]
    else:
        out_specs.append(_rows(D_A, TT))
        out_shape.append(jax.ShapeDtypeStruct((T, D_A), F32))
    tile = pltpu.VMEM((TT, D_A), F32)
    scratch = [pltpu.VMEM((HALO + TT, D_IN_A), F32)] + [tile] * 8 + [pltpu.VMEM((N_GROUPS, GROUP_W, GROUP_W), F32)]
    outs = pl.pallas_call(
        functools.partial(_rwkv_kernel, has_vres=has_vres),
        grid=(T // TT,),
        in_specs=in_specs,
        out_specs=out_specs,
        out_shape=out_shape,
        scratch_shapes=scratch,
        compiler_params=_params(),
        name="rwkv7",
    )(*args)
    return (outs[0], v_first) if has_vres else (outs[0], outs[1])


def _even_out_kernel(x_ref, ya_ref, pb_ref, cw_ref, wo_ref, o_ref, ext_ref):
    @pl.when(pl.program_id(0) == 0)
    def _():
        ext_ref[0:HALO, :] = jnp.zeros((HALO, D_B), F32)

    pb = pb_ref[...]
    ch = pb[:, D_B:2 * D_B] * pb[:, 2 * D_B:]
    ext_ref[HALO:HALO + TT, :] = ch
    conv = (cw_ref[0:1, :] * ext_ref[HALO - 2:HALO - 2 + TT, :]
            + cw_ref[1:2, :] * ext_ref[HALO - 1:HALO - 1 + TT, :]
            + cw_ref[2:3, :] * ch)
    ext_ref[0:HALO, :] = ch[TT - HALO:TT, :]
    yb = pb[:, :D_B] * conv
    mix = _bdot(jnp.concatenate([ya_ref[...], yb], axis=-1), wo_ref[...])
    o_ref[...] = x_ref[...] + mix


def _even_out(x, ya, pb, conv_w, w_out):
    T = x.shape[0]
    return pl.pallas_call(
        _even_out_kernel,
        grid=(T // TT,),
        in_specs=[_rows(D_MODEL, TT), _rows(D_A, TT), _rows(3 * D_B, TT), _resident(conv_w.shape),
                  _resident((D_MODEL, D_MODEL))],
        out_specs=_rows(D_MODEL, TT),
        out_shape=jax.ShapeDtypeStruct((T, D_MODEL), F32),
        scratch_shapes=[pltpu.VMEM((HALO + TT, D_B), F32)],
        compiler_params=_params(),
        name="even_out",
    )(x, ya, pb, conv_w, w_out)


def _gelu_tanh(x):
    return x * (0.5 * (1.0 + jnp.tanh(0.7978845608028654 * (x + 0.044715 * (x * x * x)))))


def _odd_kernel(x_ref, g_ref, win_ref, cw_ref, cb_ref, wa_ref, ba_ref, wx_ref, bx_ref, lam_ref, dw_ref, ds_ref,
                wo_ref, o_ref, extu_ref, extd_ref, h_ref):
    i = pl.program_id(0)

    @pl.when(i == 0)
    def _():
        extu_ref[0:HALO, :] = jnp.zeros((HALO, D_C), F32)
        extd_ref[0:POOL_HALO, :] = jnp.zeros((POOL_HALO, D_D), F32)
        h_ref[...] = jnp.zeros_like(h_ref)

    x = x_ref[...]
    h = _rmsnorm(x, g_ref[...]).astype(BF16)
    pc = jnp.dot(h, win_ref[:, :2 * D_C], preferred_element_type=F32)
    pd = jnp.dot(h, win_ref[:, 2 * D_C:], preferred_element_type=F32)
    t_glob = i * TT + lax.broadcasted_iota(jnp.int32, (TT, LANES), 0)

    gate = pc[:, :D_C]
    u_in = pc[:, D_C:]
    extu_ref[HALO:HALO + TT, :] = u_in
    u = cb_ref[...] + cw_ref[3:4, :] * u_in
    for j in range(3):
        u = u + cw_ref[j:j + 1, :] * extu_ref[HALO - 3 + j:HALO - 3 + j + TT, :]
    extu_ref[0:HALO, :] = u_in[TT - HALO:TT, :]
    rec = _sigmoid(_bdot(u, wa_ref[...]) + ba_ref[...])
    inp = _sigmoid(_bdot(u, wx_ref[...]) + bx_ref[...])
    log_a = (-LRU_C) * rec * _softplus(-lam_ref[...])
    a = jnp.exp(log_a)
    mult = jnp.sqrt(-jnp.tanh(log_a) * (a * a + 1.0))
    row = lax.broadcasted_iota(jnp.int32, (TT, D_C), 0)
    mult = jnp.where(row + i * TT == 0, 1.0, mult)
    b = mult * inp * u
    step = 1
    while step < TT:
        valid = row >= step
        a_prev = jnp.where(valid, pltpu.roll(a, step, 0), 1.0)
        b_prev = jnp.where(valid, pltpu.roll(b, step, 0), 0.0)
        b = a * b_prev + b
        a = a * a_prev
        step *= 2
    hs = a * h_ref[...] + b
    h_ref[...] = hs[TT - 1:TT, :]
    yc = _gelu_tanh(gate) * hs

    extd_ref[POOL_HALO:POOL_HALO + TT, :] = pd
    parts = []
    for gi, win in enumerate(POOL_WINDOWS):
        e = extd_ref[:, gi * G_D:(gi + 1) * G_D]
        span = 1
        while span < win:
            e = e + pltpu.roll(e, span, 0)
            span *= 2
        n_avail = jnp.minimum(t_glob + 1, win).astype(F32)
        parts.append(e[POOL_HALO:, :] / n_avail - pd[:, gi * G_D:(gi + 1) * G_D])
    extd_ref[0:POOL_HALO, :] = pd[TT - POOL_HALO:TT, :]
    yd = _bdot(jnp.concatenate(parts, axis=-1), dw_ref[...]) * ds_ref[...]

    mix = _bdot(jnp.concatenate([yc, yd], axis=-1), wo_ref[...])
    o_ref[...] = x + mix


def _odd_mix(x, g, prm):
    T = x.shape[0]
    vec = _resident((1, D_C))
    sq = _resident((D_C, D_C))
    return pl.pallas_call(
        _odd_kernel,
        grid=(T // TT,),
        in_specs=[_rows(D_MODEL, TT), _resident((1, D_MODEL)), _resident((D_MODEL, 2 * D_C + D_D)),
                  _resident(prm["conv_w"].shape), vec, sq, vec, sq, vec, vec, sq, vec,
                  _resident((D_MODEL, D_MODEL))],
        out_specs=_rows(D_MODEL, TT),
        out_shape=jax.ShapeDtypeStruct((T, D_MODEL), F32),
        scratch_shapes=[pltpu.VMEM((HALO + TT, D_C), F32), pltpu.VMEM((POOL_HALO + TT, D_D), F32),
                        pltpu.VMEM((1, D_C), F32)],
        compiler_params=_params(),
        name="odd_mix",
    )(x, g, prm["w_in"], prm["conv_w"], prm["conv_b"], prm["wa"], prm["ba"], prm["wx"], prm["bx"], prm["lam"],
      prm["dw"], prm["ds"], prm["w_out"])


def _block_diag(w):
    H, n, m = w.shape
    return jnp.einsum("hij,hg->higj", w, jnp.eye(H, dtype=w.dtype)).reshape(H * n, H * m)


def _pad_rows(w, before, total):
    return jnp.pad(w, ((before, total - before - w.shape[0]), (0, 0)))


def kernel(x, norm_g, ffn_wg, ffn_wu, ffn_wd, even_w_in, even_w_out, a_mu, a_w0, a_w_up, a_a0, a_a_up, a_g_up,
           a_k_k, a_k_a, a_r_k, a_ln_w, a_ln_b, a_v0, a_v_dn, a_v_up, b_conv_w, odd_w_in, odd_w_out, c_conv_w,
           c_conv_b, c_wa, c_ba, c_wx, c_bx, c_lam, d_w, d_scale, final_g):
    B, T, D = x.shape
    assert (B, D) == (1, D_MODEL) and T % TM_FFN == 0 and T % TT == 0
    xt = x.reshape(T, D)
    wg, wu, wd = ffn_wg.astype(BF16), ffn_wu.astype(BF16), ffn_wd.astype(BF16)
    row = lambda v: v.reshape(1, -1)
    lora_w = a_w_up.shape[1]
    seg = _block_diag(jnp.ones((D_A // HEAD_A, HEAD_A, HEAD_A), F32))
    v_first = None
    for layer in range(DEPTH):
        xt = _ffn(xt, row(norm_g[layer, 0]), wg, wu, wd, layer, 0)
        g_mix = row(norm_g[layer, 1])
        if layer % 2 == 0:
            e = layer // 2
            pa, pb = _even_proj(xt, g_mix, even_w_in[e].astype(BF16))
            prm = dict(
                mu=row(a_mu[e]), w0=row(a_w0[e]), a0=row(a_a0[e]),
                w_up=_pad_rows(a_w_up[e], 0, LANES).astype(BF16),
                a_up=_pad_rows(a_a_up[e], lora_w, LANES).astype(BF16),
                g_up=a_g_up[e].astype(BF16),
                k_k=row(a_k_k[e]), k_a=row(a_k_a[e]), r_k=row(a_r_k[e]), ln_w=row(a_ln_w[e]), ln_b=row(a_ln_b[e]),
                seg=seg)
            vres = None
            if e > 0:
                vres = dict(v0=row(a_v0[e - 1]),
                            v_dn=jnp.pad(a_v_dn[e - 1], ((0, 0), (0, LANES - LORA_V))).astype(BF16),
                            v_up=_pad_rows(a_v_up[e - 1], 0, LANES).astype(BF16))
            ya, v_first = _rwkv(pa, prm, v_first, vres)
            xt = _even_out(xt, ya, pb, b_conv_w[e], even_w_out[e].astype(BF16))
        else:
            o = layer // 2
            prm = dict(
                w_in=odd_w_in[o].astype(BF16), w_out=odd_w_out[o].astype(BF16),
                conv_w=c_conv_w[o], conv_b=row(c_conv_b[o]),
                wa=_block_diag(c_wa[o]).astype(BF16), ba=row(c_ba[o]),
                wx=_block_diag(c_wx[o]).astype(BF16), bx=row(c_bx[o]), lam=row(c_lam[o]),
                dw=_block_diag(d_w[o]).astype(BF16), ds=row(d_scale[o]))
            xt = _odd_mix(xt, g_mix, prm)
        last = layer == DEPTH - 1
        xt = _ffn(xt, row(norm_g[layer, 2]), wg, wu, wd, layer, 1, final_g=row(final_g) if last else None)
    return xt.reshape(B, T, D)
```

```python
import functools

import jax
import jax.numpy as jnp
from jax import lax
from jax.experimental import pallas as pl
from jax.experimental.pallas import tpu as pltpu

F32 = jnp.float32
BF16 = jnp.bfloat16

D_MODEL = 1024
DEPTH = 4
D_A = 512
HEAD_A = 64
LORA_V = 32
D_IN_A = 1792
D_B = 512
D_C = 512
H_C = 8
D_D = 512
POOL_WINDOWS = (2, 4, 8, 16)
G_D = 128
D_FF = 2816
GN_EPS = 64e-5
RMS_EPS = 1e-6
LRU_C = 8.0

LANES = 128
SUBLANES = 8
MXU_DIM = 256
VMEM_LIMIT_BYTES = 56 * 1024 * 1024

TM_FFN = 512
FF_CHUNK = MXU_DIM
TT = 256
CHUNK = 64
GROUP_W = MXU_DIM
N_GROUPS = D_A // GROUP_W
HALO = SUBLANES
POOL_HALO = 2 * SUBLANES

N_SCORE = 1
N_INV = 1
N_APPLY = 1
N_TRANS = 1
N_STATE = 2
N_EXACT = 3

_NN = (((1,), (0,)), ((), ()))
_NT = (((1,), (1,)), ((), ()))
_TN = (((0,), (0,)), ((), ()))


def _params(n_axes=1):
    return pltpu.CompilerParams(dimension_semantics=("arbitrary",) * n_axes,
                                vmem_limit_bytes=VMEM_LIMIT_BYTES)


def _resident(shape):
    zeros = (0,) * len(shape)
    return pl.BlockSpec(shape, lambda i: zeros, pipeline_mode=pl.Buffered(1))


def _rows(width, tile):
    return pl.BlockSpec((tile, width), lambda i: (i, 0))


def _rmsnorm(x, g):
    ms = jnp.mean(x * x, axis=-1, keepdims=True)
    return x * lax.rsqrt(ms + RMS_EPS) * g


def _sigmoid(x):
    return 1.0 / (1.0 + jnp.exp(-x))


def _softplus(z):
    return jnp.maximum(z, 0.0) + jnp.log1p(jnp.exp(-jnp.abs(z)))


def _bdot(x, w):
    return jnp.dot(x.astype(BF16), w, preferred_element_type=F32)


def _split(x, n):
    pieces = []
    for _ in range(n - 1):
        p = x.astype(BF16)
        pieces.append(p)
        x = x - p.astype(F32)
    pieces.append(x.astype(BF16))
    return pieces


def _mm_pieces(xs, ys, dims=_NN):
    n = max(len(xs), len(ys))
    axis = 1 if dims == _TN else 0
    rows = xs[0].shape[axis]
    out = None
    for j, y in enumerate(ys):
        lhs = xs[:n - j]
        if not lhs:
            continue
        stacked = lhs[0] if len(lhs) == 1 else jnp.concatenate(lhs, axis=axis)
        prod = lax.dot_general(stacked, y, dims, preferred_element_type=F32)
        for i in range(len(lhs)):
            part = prod[i * rows:(i + 1) * rows]
            out = part if out is None else out + part
    return out


def _ffn_kernel(x_ref, g_ref, wg_ref, wu_ref, wd_ref, fg_ref, o_ref, act_ref, *, final_norm):
    x = x_ref[...]
    h = _rmsnorm(x, g_ref[...]).astype(BF16)
    for j in range(D_FF // FF_CHUNK):
        cols = slice(j * FF_CHUNK, (j + 1) * FF_CHUNK)
        gate = jnp.dot(h, wg_ref[:, cols], preferred_element_type=F32)
        up = jnp.dot(h, wu_ref[:, cols], preferred_element_type=F32)
        act_ref[:, cols] = (gate * _sigmoid(gate) * up).astype(BF16)
    y = x + 0.5 * jnp.dot(act_ref[...], wd_ref[...], preferred_element_type=F32)
    if final_norm:
        y = _rmsnorm(y, fg_ref[...])
    o_ref[...] = y


def _ffn(x, g, wg, wu, wd, layer, k, final_g=None):
    T = x.shape[0]
    w_in_spec = pl.BlockSpec((None, None, D_MODEL, D_FF), lambda i: (layer, k, 0, 0), pipeline_mode=pl.Buffered(1))
    w_out_spec = pl.BlockSpec((None, None, D_FF, D_MODEL), lambda i: (layer, k, 0, 0), pipeline_mode=pl.Buffered(1))
    fg = g if final_g is None else final_g
    return pl.pallas_call(
        functools.partial(_ffn_kernel, final_norm=final_g is not None),
        grid=(T // TM_FFN,),
        in_specs=[_rows(D_MODEL, TM_FFN), _resident((1, D_MODEL)), w_in_spec, w_in_spec, w_out_spec,
                  _resident((1, D_MODEL))],
        out_specs=_rows(D_MODEL, TM_FFN),
        out_shape=jax.ShapeDtypeStruct((T, D_MODEL), F32),
        scratch_shapes=[pltpu.VMEM((TM_FFN, D_FF), BF16)],
        compiler_params=_params(),
        name="ffn",
    )(x, g, wg, wu, wd, fg)


def _proj_kernel(x_ref, g_ref, w_ref, pa_ref, pb_ref):
    h = _rmsnorm(x_ref[...], g_ref[...]).astype(BF16)
    pa_ref[...] = jnp.dot(h, w_ref[:, :D_IN_A], preferred_element_type=F32)
    pb_ref[...] = jnp.dot(h, w_ref[:, D_IN_A:], preferred_element_type=F32)


def _even_proj(x, g, w_in):
    T = x.shape[0]
    n_b = 3 * D_B
    return pl.pallas_call(
        _proj_kernel,
        grid=(T // TT,),
        in_specs=[_rows(D_MODEL, TT), _resident((1, D_MODEL)), _resident((D_MODEL, D_IN_A + n_b))],
        out_specs=[_rows(D_IN_A, TT), _rows(n_b, TT)],
        out_shape=[jax.ShapeDtypeStruct((T, D_IN_A), F32), jax.ShapeDtypeStruct((T, n_b), F32)],
        compiler_params=_params(),
        name="even_proj",
    )(x, g, w_in)


def _rwkv_chunk(ci, gi, kt_s, rt_s, bt_s, kd_s, be_s, ke_s, v_s, cum_s, y_s, st_ref, consts):
    L = CHUNK
    bd_mask, same_head, eye256, eye_ss, incl, level_masks = consts
    rows = slice(ci * L, (ci + 1) * L)
    cols = slice(gi * GROUP_W, (gi + 1) * GROUP_W)
    reps = GROUP_W // L

    def bd(pieces):
        return [jnp.concatenate([p] * reps, axis=0) * bd_mask for p in pieces]

    def per_head(xs, y, n):
        return _mm_pieces(xs, bd(_split(y, n)))

    kt = kt_s[rows, cols]
    rt = rt_s[rows, cols]
    v = v_s[rows, cols]
    kt_p = _split(kt, max(N_SCORE, N_APPLY))
    rt_p = _split(rt, N_SCORE)
    lhs_p = [jnp.concatenate([a, b], axis=0) for a, b in zip(kt_p[:N_SCORE], rt_p)]
    s_b = _mm_pieces(lhs_p, bd(_split(bt_s[rows, cols], N_SCORE)), _NT)
    s_k = _mm_pieces(lhs_p, bd(_split(kd_s[rows, cols], N_SCORE)), _NT)
    a_ab = s_b[:L]
    a_ak = jnp.where(level_masks[-1], s_k[:L], 0.0)
    a_rb = jnp.where(incl, s_b[L:], 0.0)
    a_rk = jnp.where(incl, s_k[L:], 0.0)

    t_inv = eye_ss - jnp.where(level_masks[0], a_ab, 0.0)
    for m in level_masks[1:-1]:
        t_p = _split(t_inv, N_INV)
        left = per_head(t_p, jnp.where(m, a_ab, 0.0), N_INV)
        t_inv = t_inv - _mm_pieces(_split(left, N_INV), bd(t_p))

    t_p = _split(t_inv, N_APPLY)
    v_bd = bd(_split(v, N_APPLY))
    w_mat = -_mm_pieces(t_p, bd(kt_p[:N_APPLY]))
    u0 = -per_head(t_p, _mm_pieces(_split(a_ak, N_APPLY), v_bd), N_APPLY)
    arb_p = _split(a_rb, N_APPLY)
    r_hat = rt + per_head(arb_p, w_mat, N_APPLY)
    y0 = per_head(arb_p, u0, N_APPLY) + _mm_pieces(_split(a_rk, N_APPLY), v_bd)

    be_p = _split(be_s[rows, cols], N_TRANS)
    p_end = jnp.exp(cum_s[ci * L + L - 1:ci * L + L, cols])
    m_t = (jnp.where(same_head, _mm_pieces(be_p, _split(w_mat, N_TRANS), _TN), 0.0)
           + jnp.where(eye256, p_end, 0.0))
    c_t = jnp.where(same_head, _mm_pieces(be_p, _split(u0, N_TRANS), _TN)
                    + _mm_pieces(_split(ke_s[rows, cols], N_TRANS), _split(v, N_TRANS), _TN), 0.0)

    st_p = _split(st_ref[gi], N_STATE)
    both = _mm_pieces([jnp.concatenate([a, b], axis=0)
                       for a, b in zip(_split(r_hat, N_STATE), _split(m_t, N_STATE))], st_p)
    y_s[rows, cols] = both[:L] + y0
    st_ref[gi] = both[L:] + c_t


def _chunk_constants():
    L = CHUNK
    row = lax.broadcasted_iota(jnp.int32, (L, GROUP_W), 0)
    col = lax.broadcasted_iota(jnp.int32, (L, GROUP_W), 1) & (HEAD_A - 1)
    r256 = lax.broadcasted_iota(jnp.int32, (GROUP_W, GROUP_W), 0)
    c256 = lax.broadcasted_iota(jnp.int32, (GROUP_W, GROUP_W), 1)
    same_head = (r256 // HEAD_A) == (c256 // HEAD_A)
    bd_mask = jnp.where(same_head, 1.0, 0.0).astype(BF16)
    eye256 = r256 == c256
    eye_ss = jnp.where(row == col, 1.0, 0.0)
    incl = row >= col
    level_masks = []
    s = 1
    while s < L:
        level_masks.append(((row ^ col) < 2 * s) & ((row & s) > (col & s)) & ((row ^ col) >= s))
        s *= 2
    level_masks.append(row > col)
    return bd_mask, same_head, eye256, eye_ss, incl, level_masks


def _seg_sum(x, seg):
    return _mm_pieces(_split(x, N_EXACT), [seg])


def _rwkv_kernel(*refs, has_vres):
    (p_ref, mu_ref, w0_ref, wup_ref, a0_ref, aup_ref, gup_ref, kk_ref, ka_ref, rk_ref, lnw_ref, lnb_ref,
     seg_ref) = refs[:13]
    n_in = 13
    if has_vres:
        vf_ref, v0_ref, vdn_ref, vup_ref = refs[13:17]
        n_in = 17
        ya_ref = refs[n_in]
        scratch = refs[n_in + 1:]
    else:
        ya_ref, vout_ref = refs[n_in:n_in + 2]
        scratch = refs[n_in + 2:]
    ext_ref, r_s, k_s, v_s, g_s, kt_s, rt_s, bt_s, kd_s, be_s, ke_s, cum_s, y_s, st_ref = scratch
    L = CHUNK

    @pl.when(pl.program_id(0) == 0)
    def _():
        ext_ref[0:HALO, :] = jnp.zeros((HALO, D_IN_A), F32)
        st_ref[...] = jnp.zeros_like(st_ref)

    p = p_ref[...]
    ext_ref[HALO:HALO + TT, :] = p
    prev = ext_ref[HALO - 1:HALO - 1 + TT, :]
    ext_ref[0:HALO, :] = p[TT - HALO:TT, :]
    p = p + (prev - p) * mu_ref[...]

    r = p[:, 0:D_A]
    k = p[:, D_A:2 * D_A]
    v = p[:, 2 * D_A:3 * D_A]
    x_wa = p[:, 3 * D_A:3 * D_A + LANES]
    xg = p[:, 3 * D_A + LANES:]

    w = -_softplus(-(w0_ref[...] + _bdot(jnp.tanh(x_wa), wup_ref[...]))) - 0.5
    a = _sigmoid(a0_ref[...] + _bdot(x_wa, aup_ref[...]))
    g_s[...] = _bdot(_sigmoid(xg), gup_ref[...])
    if has_vres:
        low = _bdot(v, vdn_ref[...])
        v = v + (vf_ref[...] - v) * _sigmoid(v0_ref[...] + _bdot(low, vup_ref[...]))
    else:
        vout_ref[...] = v
    kn = k * kk_ref[...]
    kn = kn * lax.rsqrt(jnp.maximum(_seg_sum(kn * kn, seg_ref[...]), 1e-24))
    k = k * (1.0 + (a - 1.0) * ka_ref[...])
    wl = -jnp.exp(w)

    ti = lax.broadcasted_iota(jnp.int32, (TT, TT), 0)
    tj = lax.broadcasted_iota(jnp.int32, (TT, TT), 1)
    same_chunk = (ti // L) == (tj // L)
    tri = jnp.concatenate([jnp.where(same_chunk, jnp.where(tj <= ti, 1.0, 0.0), 0.0),
                           jnp.where(same_chunk, jnp.where(tj > ti, 1.0, 0.0), 0.0)], axis=0).astype(BF16)
    sums = _mm_pieces([tri], _split(wl, N_EXACT))
    cum = sums[:TT]
    to_end = sums[TT:]

    b = kn * a
    inv_p = jnp.exp(-cum)
    e_end = jnp.exp(to_end)
    r_s[...] = r
    k_s[...] = k
    v_s[...] = v
    cum_s[...] = cum
    kt_s[...] = kn * jnp.exp(cum - wl)
    rt_s[...] = r * jnp.exp(cum)
    bt_s[...] = b * inv_p
    kd_s[...] = k * inv_p
    be_s[...] = b * e_end
    ke_s[...] = k * e_end

    consts = _chunk_constants()
    for ci in range(TT // L):
        for gi in range(N_GROUPS):
            _rwkv_chunk(ci, gi, kt_s, rt_s, bt_s, kd_s, be_s, ke_s, v_s, cum_s, y_s, st_ref, consts)

    y = y_s[...]
    inv_n = 1.0 / HEAD_A
    mean = _seg_sum(y, seg_ref[...]) * inv_n
    d = y - mean
    var = _seg_sum(d * d, seg_ref[...]) * inv_n
    y = d * lax.rsqrt(var + GN_EPS) * lnw_ref[...] + lnb_ref[...]
    y = y + _seg_sum(r_s[...] * k_s[...] * rk_ref[...], seg_ref[...]) * v_s[...]
    ya_ref[...] = y * g_s[...]


def _rwkv(pa, prm, v_first, vres):
    T = pa.shape[0]
    has_vres = vres is not None
    vec = _resident((1, D_A))
    lora = _resident((LANES, D_A))
    in_specs = [_rows(D_IN_A, TT), _resident((1, D_IN_A)), vec, lora, vec, lora, lora, vec, vec, vec, vec, vec,
                _resident((D_A, D_A))]
    args = [pa, prm["mu"], prm["w0"], prm["w_up"], prm["a0"], prm["a_up"], prm["g_up"], prm["k_k"], prm["k_a"],
            prm["r_k"], prm["ln_w"], prm["ln_b"], prm["seg"]]
    out_specs = [_rows(D_A, TT)]
    out_shape = [jax.ShapeDtypeStruct((T, D_A), F32)]
    if has_vres:
        in_specs += [_rows(D_A, TT), vec, _resident((D_A, LANES)), lora]
        args += [v_first, vres["v0"], vres["v_dn"], vres["v_up"]]
    else:
        out_specs.append(_rows(D_A, TT))
        out_shape.append(jax.ShapeDtypeStruct((T, D_A), F32))
    tile = pltpu.VMEM((TT, D_A), F32)
    scratch = [pltpu.VMEM((HALO + TT, D_IN_A), F32)] + [tile] * 12 + [pltpu.VMEM((N_GROUPS, GROUP_W, GROUP_W), F32)]
    outs = pl.pallas_call(
        functools.partial(_rwkv_kernel, has_vres=has_vres),
        grid=(T // TT,),
        in_specs=in_specs,
        out_specs=out_specs,
        out_shape=out_shape,
        scratch_shapes=scratch,
        compiler_params=_params(),
        name="rwkv7",
    )(*args)
    return (outs[0], v_first) if has_vres else (outs[0], outs[1])


def _even_out_kernel(x_ref, ya_ref, pb_ref, cw_ref, wo_ref, o_ref, ext_ref):
    @pl.when(pl.program_id(0) == 0)
    def _():
        ext_ref[0:HALO, :] = jnp.zeros((HALO, D_B), F32)

    pb = pb_ref[...]
    ch = pb[:, D_B:2 * D_B] * pb[:, 2 * D_B:]
    ext_ref[HALO:HALO + TT, :] = ch
    conv = (cw_ref[0:1, :] * ext_ref[HALO - 2:HALO - 2 + TT, :]
            + cw_ref[1:2, :] * ext_ref[HALO - 1:HALO - 1 + TT, :]
            + cw_ref[2:3, :] * ch)
    ext_ref[0:HALO, :] = ch[TT - HALO:TT, :]
    yb = pb[:, :D_B] * conv
    mix = _bdot(jnp.concatenate([ya_ref[...], yb], axis=-1), wo_ref[...])
    o_ref[...] = x_ref[...] + mix


def _even_out(x, ya, pb, conv_w, w_out):
    T = x.shape[0]
    return pl.pallas_call(
        _even_out_kernel,
        grid=(T // TT,),
        in_specs=[_rows(D_MODEL, TT), _rows(D_A, TT), _rows(3 * D_B, TT), _resident(conv_w.shape),
                  _resident((D_MODEL, D_MODEL))],
        out_specs=_rows(D_MODEL, TT),
        out_shape=jax.ShapeDtypeStruct((T, D_MODEL), F32),
        scratch_shapes=[pltpu.VMEM((HALO + TT, D_B), F32)],
        compiler_params=_params(),
        name="even_out",
    )(x, ya, pb, conv_w, w_out)


def _gelu_tanh(x):
    return x * (0.5 * (1.0 + jnp.tanh(0.7978845608028654 * (x + 0.044715 * (x * x * x)))))


def _odd_kernel(x_ref, g_ref, win_ref, cw_ref, cb_ref, wa_ref, ba_ref, wx_ref, bx_ref, lam_ref, dw_ref, ds_ref,
                wo_ref, o_ref, extu_ref, extd_ref, h_ref):
    i = pl.program_id(0)

    @pl.when(i == 0)
    def _():
        extu_ref[0:HALO, :] = jnp.zeros((HALO, D_C), F32)
        extd_ref[0:POOL_HALO, :] = jnp.zeros((POOL_HALO, D_D), F32)
        h_ref[...] = jnp.zeros_like(h_ref)

    x = x_ref[...]
    h = _rmsnorm(x, g_ref[...]).astype(BF16)
    pc = jnp.dot(h, win_ref[:, :2 * D_C], preferred_element_type=F32)
    pd = jnp.dot(h, win_ref[:, 2 * D_C:], preferred_element_type=F32)
    t_glob = i * TT + lax.broadcasted_iota(jnp.int32, (TT, LANES), 0)

    gate = pc[:, :D_C]
    u_in = pc[:, D_C:]
    extu_ref[HALO:HALO + TT, :] = u_in
    u = cb_ref[...] + cw_ref[3:4, :] * u_in
    for j in range(3):
        u = u + cw_ref[j:j + 1, :] * extu_ref[HALO - 3 + j:HALO - 3 + j + TT, :]
    extu_ref[0:HALO, :] = u_in[TT - HALO:TT, :]
    rec = _sigmoid(_bdot(u, wa_ref[...]) + ba_ref[...])
    inp = _sigmoid(_bdot(u, wx_ref[...]) + bx_ref[...])
    log_a = (-LRU_C) * rec * _softplus(-lam_ref[...])
    a = jnp.exp(log_a)
    mult = jnp.sqrt(-jnp.tanh(log_a) * (a * a + 1.0))
    row = lax.broadcasted_iota(jnp.int32, (TT, D_C), 0)
    mult = jnp.where(row + i * TT == 0, 1.0, mult)
    b = mult * inp * u
    step = 1
    while step < TT:
        valid = row >= step
        a_prev = jnp.where(valid, pltpu.roll(a, step, 0), 1.0)
        b_prev = jnp.where(valid, pltpu.roll(b, step, 0), 0.0)
        b = a * b_prev + b
        a = a * a_prev
        step *= 2
    hs = a * h_ref[...] + b
    h_ref[...] = hs[TT - 1:TT, :]
    yc = _gelu_tanh(gate) * hs

    extd_ref[POOL_HALO:POOL_HALO + TT, :] = pd
    parts = []
    for gi, win in enumerate(POOL_WINDOWS):
        e = extd_ref[:, gi * G_D:(gi + 1) * G_D]
        span = 1
        while span < win:
            e = e + pltpu.roll(e, span, 0)
            span *= 2
        n_avail = jnp.minimum(t_glob + 1, win).astype(F32)
        parts.append(e[POOL_HALO:, :] / n_avail - pd[:, gi * G_D:(gi + 1) * G_D])
    extd_ref[0:POOL_HALO, :] = pd[TT - POOL_HALO:TT, :]
    yd = _bdot(jnp.concatenate(parts, axis=-1), dw_ref[...]) * ds_ref[...]

    mix = _bdot(jnp.concatenate([yc, yd], axis=-1), wo_ref[...])
    o_ref[...] = x + mix


def _odd_mix(x, g, prm):
    T = x.shape[0]
    vec = _resident((1, D_C))
    sq = _resident((D_C, D_C))
    return pl.pallas_call(
        _odd_kernel,
        grid=(T // TT,),
        in_specs=[_rows(D_MODEL, TT), _resident((1, D_MODEL)), _resident((D_MODEL, 2 * D_C + D_D)),
                  _resident(prm["conv_w"].shape), vec, sq, vec, sq, vec, vec, sq, vec,
                  _resident((D_MODEL, D_MODEL))],
        out_specs=_rows(D_MODEL, TT),
        out_shape=jax.ShapeDtypeStruct((T, D_MODEL), F32),
        scratch_shapes=[pltpu.VMEM((HALO + TT, D_C), F32), pltpu.VMEM((POOL_HALO + TT, D_D), F32),
                        pltpu.VMEM((1, D_C), F32)],
        compiler_params=_params(),
        name="odd_mix",
    )(x, g, prm["w_in"], prm["conv_w"], prm["conv_b"], prm["wa"], prm["ba"], prm["wx"], prm["bx"], prm["lam"],
      prm["dw"], prm["ds"], prm["w_out"])


def _block_diag(w):
    H, n, m = w.shape
    return jnp.einsum("hij,hg->higj", w, jnp.eye(H, dtype=w.dtype)).reshape(H * n, H * m)


def _pad_rows(w, before, total):
    return jnp.pad(w, ((before, total - before - w.shape[0]), (0, 0)))


def kernel(x, norm_g, ffn_wg, ffn_wu, ffn_wd, even_w_in, even_w_out, a_mu, a_w0, a_w_up, a_a0, a_a_up, a_g_up,
           a_k_k, a_k_a, a_r_k, a_ln_w, a_ln_b, a_v0, a_v_dn, a_v_up, b_conv_w, odd_w_in, odd_w_out, c_conv_w,
           c_conv_b, c_wa, c_ba, c_wx, c_bx, c_lam, d_w, d_scale, final_g):
    B, T, D = x.shape
    assert (B, D) == (1, D_MODEL) and T % TM_FFN == 0 and T % TT == 0
    xt = x.reshape(T, D)
    wg, wu, wd = ffn_wg.astype(BF16), ffn_wu.astype(BF16), ffn_wd.astype(BF16)
    row = lambda v: v.reshape(1, -1)
    lora_w = a_w_up.shape[1]
    seg = _block_diag(jnp.ones((D_A // HEAD_A, HEAD_A, HEAD_A), BF16))
    v_first = None
    for layer in range(DEPTH):
        xt = _ffn(xt, row(norm_g[layer, 0]), wg, wu, wd, layer, 0)
        g_mix = row(norm_g[layer, 1])
        if layer % 2 == 0:
            e = layer // 2
            pa, pb = _even_proj(xt, g_mix, even_w_in[e].astype(BF16))
            prm = dict(
                mu=row(a_mu[e]), w0=row(a_w0[e]), a0=row(a_a0[e]),
                w_up=_pad_rows(a_w_up[e], 0, LANES).astype(BF16),
                a_up=_pad_rows(a_a_up[e], lora_w, LANES).astype(BF16),
                g_up=a_g_up[e].astype(BF16),
                k_k=row(a_k_k[e]), k_a=row(a_k_a[e]), r_k=row(a_r_k[e]), ln_w=row(a_ln_w[e]), ln_b=row(a_ln_b[e]),
                seg=seg)
            vres = None
            if e > 0:
                vres = dict(v0=row(a_v0[e - 1]),
                            v_dn=jnp.pad(a_v_dn[e - 1], ((0, 0), (0, LANES - LORA_V))).astype(BF16),
                            v_up=_pad_rows(a_v_up[e - 1], 0, LANES).astype(BF16))
            ya, v_first = _rwkv(pa, prm, v_first, vres)
            xt = _even_out(xt, ya, pb, b_conv_w[e], even_w_out[e].astype(BF16))
        else:
            o = layer // 2
            prm = dict(
                w_in=odd_w_in[o].astype(BF16), w_out=odd_w_out[o].astype(BF16),
                conv_w=c_conv_w[o], conv_b=row(c_conv_b[o]),
                wa=_block_diag(c_wa[o]).astype(BF16), ba=row(c_ba[o]),
                wx=_block_diag(c_wx[o]).astype(BF16), bx=row(c_bx[o]), lam=row(c_lam[o]),
                dw=_block_diag(d_w[o]).astype(BF16), ds=row(d_scale[o]))
            xt = _odd_mix(xt, g_mix, prm)
        last = layer == DEPTH - 1
        xt = _ffn(xt, row(norm_g[layer, 2]), wg, wu, wd, layer, 1, final_g=row(final_g) if last else None)
    return xt.reshape(B, T, D)
```

```python
import functools

import jax
import jax.numpy as jnp
from jax import lax
from jax.experimental import pallas as pl
from jax.experimental.pallas import tpu as pltpu

F32 = jnp.float32
BF16 = jnp.bfloat16

D_MODEL = 1024
DEPTH = 4
D_A = 512
HEAD_A = 64
LORA_V = 32
D_IN_A = 1792
D_B = 512
D_C = 512
H_C = 8
D_D = 512
POOL_WINDOWS = (2, 4, 8, 16)
G_D = 128
D_FF = 2816
GN_EPS = 64e-5
RMS_EPS = 1e-6
LRU_C = 8.0

LANES = 128
SUBLANES = 8
MXU_DIM = 256
VMEM_LIMIT_BYTES = 56 * 1024 * 1024

TM_FFN = 512
FF_CHUNK = MXU_DIM
TT = 256
CHUNK = 64
GROUP_W = MXU_DIM
N_GROUPS = D_A // GROUP_W
HALO = SUBLANES
POOL_HALO = 2 * SUBLANES

N_SCORE = 1
N_INV = 1
N_APPLY = 1
N_TRANS = 1
N_STATE = 2
N_EXACT = 3

_NN = (((1,), (0,)), ((), ()))
_NT = (((1,), (1,)), ((), ()))
_TN = (((0,), (0,)), ((), ()))


def _params(n_axes=1):
    return pltpu.CompilerParams(dimension_semantics=("arbitrary",) * n_axes,
                                vmem_limit_bytes=VMEM_LIMIT_BYTES)


def _resident(shape):
    zeros = (0,) * len(shape)
    return pl.BlockSpec(shape, lambda i: zeros, pipeline_mode=pl.Buffered(1))


def _rows(width, tile):
    return pl.BlockSpec((tile, width), lambda i: (i, 0))


def _rmsnorm(x, g):
    ms = jnp.mean(x * x, axis=-1, keepdims=True)
    return x * lax.rsqrt(ms + RMS_EPS) * g


def _sigmoid(x):
    return 1.0 / (1.0 + jnp.exp(-x))


def _softplus(z):
    return jnp.maximum(z, 0.0) + jnp.log1p(jnp.exp(-jnp.abs(z)))


def _bdot(x, w):
    return jnp.dot(x.astype(BF16), w, preferred_element_type=F32)


def _split(x, n):
    pieces = []
    for _ in range(n - 1):
        p = x.astype(BF16)
        pieces.append(p)
        x = x - p.astype(F32)
    pieces.append(x.astype(BF16))
    return pieces


def _mm_pieces(xs, ys, dims=_NN):
    n = max(len(xs), len(ys))
    axis = 1 if dims == _TN else 0
    rows = xs[0].shape[axis]
    out = None
    for j, y in enumerate(ys):
        lhs = xs[:n - j]
        if not lhs:
            continue
        stacked = lhs[0] if len(lhs) == 1 else jnp.concatenate(lhs, axis=axis)
        prod = lax.dot_general(stacked, y, dims, preferred_element_type=F32)
        for i in range(len(lhs)):
            part = prod[i * rows:(i + 1) * rows]
            out = part if out is None else out + part
    return out


def _ffn_kernel(x_ref, g_ref, wg_ref, wu_ref, wd_ref, fg_ref, o_ref, act_ref, *, final_norm):
    x = x_ref[...]
    h = _rmsnorm(x, g_ref[...]).astype(BF16)
    for j in range(D_FF // FF_CHUNK):
        cols = slice(j * FF_CHUNK, (j + 1) * FF_CHUNK)
        gate = jnp.dot(h, wg_ref[:, cols], preferred_element_type=F32)
        up = jnp.dot(h, wu_ref[:, cols], preferred_element_type=F32)
        act_ref[:, cols] = (gate * _sigmoid(gate) * up).astype(BF16)
    y = x + 0.5 * jnp.dot(act_ref[...], wd_ref[...], preferred_element_type=F32)
    if final_norm:
        y = _rmsnorm(y, fg_ref[...])
    o_ref[...] = y


def _ffn(x, g, wg, wu, wd, layer, k, final_g=None):
    T = x.shape[0]
    w_in_spec = pl.BlockSpec((None, None, D_MODEL, D_FF), lambda i: (layer, k, 0, 0), pipeline_mode=pl.Buffered(1))
    w_out_spec = pl.BlockSpec((None, None, D_FF, D_MODEL), lambda i: (layer, k, 0, 0), pipeline_mode=pl.Buffered(1))
    fg = g if final_g is None else final_g
    return pl.pallas_call(
        functools.partial(_ffn_kernel, final_norm=final_g is not None),
        grid=(T // TM_FFN,),
        in_specs=[_rows(D_MODEL, TM_FFN), _resident((1, D_MODEL)), w_in_spec, w_in_spec, w_out_spec,
                  _resident((1, D_MODEL))],
        out_specs=_rows(D_MODEL, TM_FFN),
        out_shape=jax.ShapeDtypeStruct((T, D_MODEL), F32),
        scratch_shapes=[pltpu.VMEM((TM_FFN, D_FF), BF16)],
        compiler_params=_params(),
        name="ffn",
    )(x, g, wg, wu, wd, fg)


def _proj_kernel(x_ref, g_ref, w_ref, pa_ref, pb_ref):
    h = _rmsnorm(x_ref[...], g_ref[...]).astype(BF16)
    pa_ref[...] = jnp.dot(h, w_ref[:, :D_IN_A], preferred_element_type=F32)
    pb_ref[...] = jnp.dot(h, w_ref[:, D_IN_A:], preferred_element_type=F32)


def _even_proj(x, g, w_in):
    T = x.shape[0]
    n_b = 3 * D_B
    return pl.pallas_call(
        _proj_kernel,
        grid=(T // TT,),
        in_specs=[_rows(D_MODEL, TT), _resident((1, D_MODEL)), _resident((D_MODEL, D_IN_A + n_b))],
        out_specs=[_rows(D_IN_A, TT), _rows(n_b, TT)],
        out_shape=[jax.ShapeDtypeStruct((T, D_IN_A), F32), jax.ShapeDtypeStruct((T, n_b), F32)],
        compiler_params=_params(),
        name="even_proj",
    )(x, g, w_in)


def _rwkv_chunks(kt_s, rt_s, bt_s, kd_s, be_s, ke_s, v_s, cum_s, y_s, st_ref):
    L = CHUNK
    reps = GROUP_W // L
    row = lax.broadcasted_iota(jnp.int32, (L, GROUP_W), 0)
    col = lax.broadcasted_iota(jnp.int32, (L, GROUP_W), 1) & (HEAD_A - 1)
    r256 = lax.broadcasted_iota(jnp.int32, (GROUP_W, GROUP_W), 0)
    c256 = lax.broadcasted_iota(jnp.int32, (GROUP_W, GROUP_W), 1)
    same_head = (r256 // HEAD_A) == (c256 // HEAD_A)
    bd_mask = jnp.where(same_head, 1.0, 0.0).astype(BF16)
    eye_ss = jnp.where(row == col, 1.0, 0.0)
    strict = row > col
    incl = row >= col
    level_masks = []
    s = 1
    while s < L:
        level_masks.append(((row ^ col) < 2 * s) & ((row & s) > (col & s)))
        s *= 2

    def bd(pieces):
        return [jnp.concatenate([p] * reps, axis=0) * bd_mask for p in pieces]

    def per_head(xs, y, n):
        return _mm_pieces(xs, bd(_split(y, n)))

    def fold(full):
        masked = jnp.where(same_head, full, 0.0)
        return sum(masked[h * L:(h + 1) * L] for h in range(reps))

    cgs = [(ci, gi) for ci in range(TT // L) for gi in range(N_GROUPS)]
    sl = {cg: (slice(cg[0] * L, (cg[0] + 1) * L), slice(cg[1] * GROUP_W, (cg[1] + 1) * GROUP_W)) for cg in cgs}
    kt_p, a_ab, a_ak, a_rb, a_rk, v_bd = {}, {}, {}, {}, {}, {}
    for cg in cgs:
        kt_p[cg] = _split(kt_s[sl[cg]], max(N_SCORE, N_APPLY))
        rt_p = _split(rt_s[sl[cg]], N_SCORE)
        lhs_p = [jnp.concatenate([a, b], axis=0) for a, b in zip(kt_p[cg][:N_SCORE], rt_p)]
        s_b = _mm_pieces(lhs_p, bd(_split(bt_s[sl[cg]], N_SCORE)), _NT)
        s_k = _mm_pieces(lhs_p, bd(_split(kd_s[sl[cg]], N_SCORE)), _NT)
        a_ab[cg] = s_b[:L]
        a_ak[cg] = jnp.where(strict, s_k[:L], 0.0)
        a_rb[cg] = jnp.where(incl, s_b[L:], 0.0)
        a_rk[cg] = jnp.where(incl, s_k[L:], 0.0)

    t_inv = {cg: eye_ss - jnp.where(level_masks[0], a_ab[cg], 0.0) for cg in cgs}
    for m in level_masks[1:]:
        t_p = {cg: _split(t_inv[cg], N_INV) for cg in cgs}
        left = {cg: per_head(t_p[cg], jnp.where(m, a_ab[cg], 0.0), N_INV) for cg in cgs}
        t_inv = {cg: t_inv[cg] - _mm_pieces(_split(left[cg], N_INV), bd(t_p[cg])) for cg in cgs}

    t_p = {cg: _split(t_inv[cg], N_APPLY) for cg in cgs}
    for cg in cgs:
        v_bd[cg] = bd(_split(v_s[sl[cg]], N_APPLY))
    w_mat = {cg: -_mm_pieces(t_p[cg], bd(kt_p[cg][:N_APPLY])) for cg in cgs}
    akv = {cg: _mm_pieces(_split(a_ak[cg], N_APPLY), v_bd[cg]) for cg in cgs}
    u0 = {cg: -per_head(t_p[cg], akv[cg], N_APPLY) for cg in cgs}
    arb_p = {cg: _split(a_rb[cg], N_APPLY) for cg in cgs}
    r_hat = {cg: rt_s[sl[cg]] + per_head(arb_p[cg], w_mat[cg], N_APPLY) for cg in cgs}
    y0 = {cg: per_head(arb_p[cg], u0[cg], N_APPLY) + _mm_pieces(_split(a_rk[cg], N_APPLY), v_bd[cg])
          for cg in cgs}

    m_ss, c_ss = {}, {}
    for cg in cgs:
        be_p = _split(be_s[sl[cg]], N_TRANS)
        ci, gi = cg
        p_end = jnp.exp(cum_s[ci * L + L - 1:ci * L + L, sl[cg][1]])
        m_ss[cg] = fold(_mm_pieces(be_p, _split(w_mat[cg], N_TRANS), _TN)) + eye_ss * p_end
        c_ss[cg] = fold(_mm_pieces(be_p, _split(u0[cg], N_TRANS), _TN)
                        + _mm_pieces(_split(ke_s[sl[cg]], N_TRANS), _split(v_s[sl[cg]], N_TRANS), _TN))

    for cg in cgs:
        gi = cg[1]
        st_bd = bd(_split(st_ref[gi], N_STATE))
        both = _mm_pieces([jnp.concatenate([a, b], axis=0)
                           for a, b in zip(_split(r_hat[cg], N_STATE), _split(m_ss[cg], N_STATE))], st_bd)
        y_s[sl[cg]] = both[:L] + y0[cg]
        st_ref[gi] = both[L:] + c_ss[cg]


def _seg_sum(x, seg):
    return _mm_pieces(_split(x, N_EXACT), [seg])


def _rwkv_kernel(*refs, has_vres):
    (p_ref, mu_ref, w0_ref, wup_ref, a0_ref, aup_ref, gup_ref, kk_ref, ka_ref, rk_ref, lnw_ref, lnb_ref,
     seg_ref) = refs[:13]
    n_in = 13
    if has_vres:
        vf_ref, v0_ref, vdn_ref, vup_ref = refs[13:17]
        n_in = 17
        ya_ref = refs[n_in]
        scratch = refs[n_in + 1:]
    else:
        ya_ref, vout_ref = refs[n_in:n_in + 2]
        scratch = refs[n_in + 2:]
    ext_ref, r_s, k_s, v_s, g_s, kt_s, rt_s, bt_s, kd_s, be_s, ke_s, cum_s, y_s, st_ref = scratch
    L = CHUNK

    @pl.when(pl.program_id(0) == 0)
    def _():
        ext_ref[0:HALO, :] = jnp.zeros((HALO, D_IN_A), F32)
        st_ref[...] = jnp.zeros_like(st_ref)

    p = p_ref[...]
    ext_ref[HALO:HALO + TT, :] = p
    prev = ext_ref[HALO - 1:HALO - 1 + TT, :]
    ext_ref[0:HALO, :] = p[TT - HALO:TT, :]
    p = p + (prev - p) * mu_ref[...]

    r = p[:, 0:D_A]
    k = p[:, D_A:2 * D_A]
    v = p[:, 2 * D_A:3 * D_A]
    x_wa = p[:, 3 * D_A:3 * D_A + LANES]
    xg = p[:, 3 * D_A + LANES:]

    w = -_softplus(-(w0_ref[...] + _bdot(jnp.tanh(x_wa), wup_ref[...]))) - 0.5
    a = _sigmoid(a0_ref[...] + _bdot(x_wa, aup_ref[...]))
    g_s[...] = _bdot(_sigmoid(xg), gup_ref[...])
    if has_vres:
        low = _bdot(v, vdn_ref[...])
        v = v + (vf_ref[...] - v) * _sigmoid(v0_ref[...] + _bdot(low, vup_ref[...]))
    else:
        vout_ref[...] = v
    kn = k * kk_ref[...]
    kn = kn * lax.rsqrt(jnp.maximum(_seg_sum(kn * kn, seg_ref[...]), 1e-24))
    k = k * (1.0 + (a - 1.0) * ka_ref[...])
    wl = -jnp.exp(w)

    ti = lax.broadcasted_iota(jnp.int32, (TT, TT), 0)
    tj = lax.broadcasted_iota(jnp.int32, (TT, TT), 1)
    same_chunk = (ti // L) == (tj // L)
    tri = jnp.concatenate([jnp.where(same_chunk, jnp.where(tj <= ti, 1.0, 0.0), 0.0),
                           jnp.where(same_chunk, jnp.where(tj > ti, 1.0, 0.0), 0.0)], axis=0).astype(BF16)
    sums = _mm_pieces([tri], _split(wl, N_EXACT))
    cum = sums[:TT]
    to_end = sums[TT:]

    b = kn * a
    inv_p = jnp.exp(-cum)
    e_end = jnp.exp(to_end)
    r_s[...] = r
    k_s[...] = k
    v_s[...] = v
    cum_s[...] = cum
    kt_s[...] = kn * jnp.exp(cum - wl)
    rt_s[...] = r * jnp.exp(cum)
    bt_s[...] = b * inv_p
    kd_s[...] = k * inv_p
    be_s[...] = b * e_end
    ke_s[...] = k * e_end

    _rwkv_chunks(kt_s, rt_s, bt_s, kd_s, be_s, ke_s, v_s, cum_s, y_s, st_ref)

    y = y_s[...]
    inv_n = 1.0 / HEAD_A
    mean = _seg_sum(y, seg_ref[...]) * inv_n
    d = y - mean
    var = _seg_sum(d * d, seg_ref[...]) * inv_n
    y = d * lax.rsqrt(var + GN_EPS) * lnw_ref[...] + lnb_ref[...]
    y = y + _seg_sum(r_s[...] * k_s[...] * rk_ref[...], seg_ref[...]) * v_s[...]
    ya_ref[...] = y * g_s[...]


def _rwkv(pa, prm, v_first, vres):
    T = pa.shape[0]
    has_vres = vres is not None
    vec = _resident((1, D_A))
    lora = _resident((LANES, D_A))
    in_specs = [_rows(D_IN_A, TT), _resident((1, D_IN_A)), vec, lora, vec, lora, lora, vec, vec, vec, vec, vec,
                _resident((D_A, D_A))]
    args = [pa, prm["mu"], prm["w0"], prm["w_up"], prm["a0"], prm["a_up"], prm["g_up"], prm["k_k"], prm["k_a"],
            prm["r_k"], prm["ln_w"], prm["ln_b"], prm["seg"]]
    out_specs = [_rows(D_A, TT)]
    out_shape = [jax.ShapeDtypeStruct((T, D_A), F32)]
    if has_vres:
        in_specs += [_rows(D_A, TT), vec, _resident((D_A, LANES)), lora]
        args += [v_first, vres["v0"], vres["v_dn"], vres["v_up"]]
    else:
        out_specs.append(_rows(D_A, TT))
        out_shape.append(jax.ShapeDtypeStruct((T, D_A), F32))
    tile = pltpu.VMEM((TT, D_A), F32)
    scratch = [pltpu.VMEM((HALO + TT, D_IN_A), F32)] + [tile] * 12 + [pltpu.VMEM((N_GROUPS, CHUNK, GROUP_W), F32)]
    outs = pl.pallas_call(
        functools.partial(_rwkv_kernel, has_vres=has_vres),
        grid=(T // TT,),
        in_specs=in_specs,
        out_specs=out_specs,
        out_shape=out_shape,
        scratch_shapes=scratch,
        compiler_params=_params(),
        name="rwkv7",
    )(*args)
    return (outs[0], v_first) if has_vres else (outs[0], outs[1])


def _even_out_kernel(x_ref, ya_ref, pb_ref, cw_ref, wo_ref, o_ref, ext_ref):
    @pl.when(pl.program_id(0) == 0)
    def _():
        ext_ref[0:HALO, :] = jnp.zeros((HALO, D_B), F32)

    pb = pb_ref[...]
    ch = pb[:, D_B:2 * D_B] * pb[:, 2 * D_B:]
    ext_ref[HALO:HALO + TT, :] = ch
    conv = (cw_ref[0:1, :] * ext_ref[HALO - 2:HALO - 2 + TT, :]
            + cw_ref[1:2, :] * ext_ref[HALO - 1:HALO - 1 + TT, :]
            + cw_ref[2:3, :] * ch)
    ext_ref[0:HALO, :] = ch[TT - HALO:TT, :]
    yb = pb[:, :D_B] * conv
    mix = _bdot(jnp.concatenate([ya_ref[...], yb], axis=-1), wo_ref[...])
    o_ref[...] = x_ref[...] + mix


def _even_out(x, ya, pb, conv_w, w_out):
    T = x.shape[0]
    return pl.pallas_call(
        _even_out_kernel,
        grid=(T // TT,),
        in_specs=[_rows(D_MODEL, TT), _rows(D_A, TT), _rows(3 * D_B, TT), _resident(conv_w.shape),
                  _resident((D_MODEL, D_MODEL))],
        out_specs=_rows(D_MODEL, TT),
        out_shape=jax.ShapeDtypeStruct((T, D_MODEL), F32),
        scratch_shapes=[pltpu.VMEM((HALO + TT, D_B), F32)],
        compiler_params=_params(),
        name="even_out",
    )(x, ya, pb, conv_w, w_out)


def _gelu_tanh(x):
    return x * (0.5 * (1.0 + jnp.tanh(0.7978845608028654 * (x + 0.044715 * (x * x * x)))))


def _odd_kernel(x_ref, g_ref, win_ref, cw_ref, cb_ref, wa_ref, ba_ref, wx_ref, bx_ref, lam_ref, dw_ref, ds_ref,
                wo_ref, o_ref, extu_ref, extd_ref, h_ref):
    i = pl.program_id(0)

    @pl.when(i == 0)
    def _():
        extu_ref[0:HALO, :] = jnp.zeros((HALO, D_C), F32)
        extd_ref[0:POOL_HALO, :] = jnp.zeros((POOL_HALO, D_D), F32)
        h_ref[...] = jnp.zeros_like(h_ref)

    x = x_ref[...]
    h = _rmsnorm(x, g_ref[...]).astype(BF16)
    pc = jnp.dot(h, win_ref[:, :2 * D_C], preferred_element_type=F32)
    pd = jnp.dot(h, win_ref[:, 2 * D_C:], preferred_element_type=F32)
    t_glob = i * TT + lax.broadcasted_iota(jnp.int32, (TT, LANES), 0)

    gate = pc[:, :D_C]
    u_in = pc[:, D_C:]
    extu_ref[HALO:HALO + TT, :] = u_in
    u = cb_ref[...] + cw_ref[3:4, :] * u_in
    for j in range(3):
        u = u + cw_ref[j:j + 1, :] * extu_ref[HALO - 3 + j:HALO - 3 + j + TT, :]
    extu_ref[0:HALO, :] = u_in[TT - HALO:TT, :]
    rec = _sigmoid(_bdot(u, wa_ref[...]) + ba_ref[...])
    inp = _sigmoid(_bdot(u, wx_ref[...]) + bx_ref[...])
    log_a = (-LRU_C) * rec * _softplus(-lam_ref[...])
    a = jnp.exp(log_a)
    mult = jnp.sqrt(-jnp.tanh(log_a) * (a * a + 1.0))
    row = lax.broadcasted_iota(jnp.int32, (TT, D_C), 0)
    mult = jnp.where(row + i * TT == 0, 1.0, mult)
    b = mult * inp * u
    step = 1
    while step < TT:
        valid = row >= step
        a_prev = jnp.where(valid, pltpu.roll(a, step, 0), 1.0)
        b_prev = jnp.where(valid, pltpu.roll(b, step, 0), 0.0)
        b = a * b_prev + b
        a = a * a_prev
        step *= 2
    hs = a * h_ref[...] + b
    h_ref[...] = hs[TT - 1:TT, :]
    yc = _gelu_tanh(gate) * hs

    extd_ref[POOL_HALO:POOL_HALO + TT, :] = pd
    parts = []
    for gi, win in enumerate(POOL_WINDOWS):
        e = extd_ref[:, gi * G_D:(gi + 1) * G_D]
        span = 1
        while span < win:
            e = e + pltpu.roll(e, span, 0)
            span *= 2
        n_avail = jnp.minimum(t_glob + 1, win).astype(F32)
        parts.append(e[POOL_HALO:, :] / n_avail - pd[:, gi * G_D:(gi + 1) * G_D])
    extd_ref[0:POOL_HALO, :] = pd[TT - POOL_HALO:TT, :]
    yd = _bdot(jnp.concatenate(parts, axis=-1), dw_ref[...]) * ds_ref[...]

    mix = _bdot(jnp.concatenate([yc, yd], axis=-1), wo_ref[...])
    o_ref[...] = x + mix


def _odd_mix(x, g, prm):
    T = x.shape[0]
    vec = _resident((1, D_C))
    sq = _resident((D_C, D_C))
    return pl.pallas_call(
        _odd_kernel,
        grid=(T // TT,),
        in_specs=[_rows(D_MODEL, TT), _resident((1, D_MODEL)), _resident((D_MODEL, 2 * D_C + D_D)),
                  _resident(prm["conv_w"].shape), vec, sq, vec, sq, vec, vec, sq, vec,
                  _resident((D_MODEL, D_MODEL))],
        out_specs=_rows(D_MODEL, TT),
        out_shape=jax.ShapeDtypeStruct((T, D_MODEL), F32),
        scratch_shapes=[pltpu.VMEM((HALO + TT, D_C), F32), pltpu.VMEM((POOL_HALO + TT, D_D), F32),
                        pltpu.VMEM((1, D_C), F32)],
        compiler_params=_params(),
        name="odd_mix",
    )(x, g, prm["w_in"], prm["conv_w"], prm["conv_b"], prm["wa"], prm["ba"], prm["wx"], prm["bx"], prm["lam"],
      prm["dw"], prm["ds"], prm["w_out"])


def _block_diag(w):
    H, n, m = w.shape
    return jnp.einsum("hij,hg->higj", w, jnp.eye(H, dtype=w.dtype)).reshape(H * n, H * m)


def _pad_rows(w, before, total):
    return jnp.pad(w, ((before, total - before - w.shape[0]), (0, 0)))


def kernel(x, norm_g, ffn_wg, ffn_wu, ffn_wd, even_w_in, even_w_out, a_mu, a_w0, a_w_up, a_a0, a_a_up, a_g_up,
           a_k_k, a_k_a, a_r_k, a_ln_w, a_ln_b, a_v0, a_v_dn, a_v_up, b_conv_w, odd_w_in, odd_w_out, c_conv_w,
           c_conv_b, c_wa, c_ba, c_wx, c_bx, c_lam, d_w, d_scale, final_g):
    B, T, D = x.shape
    assert (B, D) == (1, D_MODEL) and T % TM_FFN == 0 and T % TT == 0
    xt = x.reshape(T, D)
    wg, wu, wd = ffn_wg.astype(BF16), ffn_wu.astype(BF16), ffn_wd.astype(BF16)
    row = lambda v: v.reshape(1, -1)
    lora_w = a_w_up.shape[1]
    seg = _block_diag(jnp.ones((D_A // HEAD_A, HEAD_A, HEAD_A), BF16))
    v_first = None
    for layer in range(DEPTH):
        xt = _ffn(xt, row(norm_g[layer, 0]), wg, wu, wd, layer, 0)
        g_mix = row(norm_g[layer, 1])
        if layer % 2 == 0:
            e = layer // 2
            pa, pb = _even_proj(xt, g_mix, even_w_in[e].astype(BF16))
            prm = dict(
                mu=row(a_mu[e]), w0=row(a_w0[e]), a0=row(a_a0[e]),
                w_up=_pad_rows(a_w_up[e], 0, LANES).astype(BF16),
                a_up=_pad_rows(a_a_up[e], lora_w, LANES).astype(BF16),
                g_up=a_g_up[e].astype(BF16),
                k_k=row(a_k_k[e]), k_a=row(a_k_a[e]), r_k=row(a_r_k[e]), ln_w=row(a_ln_w[e]), ln_b=row(a_ln_b[e]),
                seg=seg)
            vres = None
            if e > 0:
                vres = dict(v0=row(a_v0[e - 1]),
                            v_dn=jnp.pad(a_v_dn[e - 1], ((0, 0), (0, LANES - LORA_V))).astype(BF16),
                            v_up=_pad_rows(a_v_up[e - 1], 0, LANES).astype(BF16))
            ya, v_first = _rwkv(pa, prm, v_first, vres)
            xt = _even_out(xt, ya, pb, b_conv_w[e], even_w_out[e].astype(BF16))
        else:
            o = layer // 2
            prm = dict(
                w_in=odd_w_in[o].astype(BF16), w_out=odd_w_out[o].astype(BF16),
                conv_w=c_conv_w[o], conv_b=row(c_conv_b[o]),
                wa=_block_diag(c_wa[o]).astype(BF16), ba=row(c_ba[o]),
                wx=_block_diag(c_wx[o]).astype(BF16), bx=row(c_bx[o]), lam=row(c_lam[o]),
                dw=_block_diag(d_w[o]).astype(BF16), ds=row(d_scale[o]))
            xt = _odd_mix(xt, g_mix, prm)
        last = layer == DEPTH - 1
        xt = _ffn(xt, row(norm_g[layer, 2]), wg, wu, wd, layer, 1, final_g=row(final_g) if last else None)
    return xt.reshape(B, T, D)
```

```python
import functools

import jax
import jax.numpy as jnp
from jax import lax
from jax.experimental import pallas as pl
from jax.experimental.pallas import tpu as pltpu

F32 = jnp.float32
BF16 = jnp.bfloat16

D_MODEL = 1024
DEPTH = 4
D_A = 512
HEAD_A = 64
LORA_V = 32
D_IN_A = 1792
D_B = 512
D_C = 512
H_C = 8
D_D = 512
POOL_WINDOWS = (2, 4, 8, 16)
G_D = 128
D_FF = 2816
GN_EPS = 64e-5
RMS_EPS = 1e-6
LRU_C = 8.0

LANES = 128
SUBLANES = 8
MXU_DIM = 256
VMEM_LIMIT_BYTES = 56 * 1024 * 1024

TM_FFN = 1024
FF_CHUNK = MXU_DIM
TT = 256
CHUNK = 64
GROUP_W = MXU_DIM
N_GROUPS = D_A // GROUP_W
HALO = SUBLANES
POOL_HALO = 2 * SUBLANES

N_SCORE = 1
N_INV = 1
N_APPLY = 1
N_TRANS = 1
N_STATE = 2
N_SUM = 2

_NN = (((1,), (0,)), ((), ()))
_NT = (((1,), (1,)), ((), ()))
_TN = (((0,), (0,)), ((), ()))


def _params(n_axes=1):
    return pltpu.CompilerParams(dimension_semantics=("arbitrary",) * n_axes,
                                vmem_limit_bytes=VMEM_LIMIT_BYTES)


def _resident(shape):
    zeros = (0,) * len(shape)
    return pl.BlockSpec(shape, lambda i: zeros, pipeline_mode=pl.Buffered(1))


def _rows(width, tile):
    return pl.BlockSpec((tile, width), lambda i: (i, 0))


def _rmsnorm(x, g):
    ms = jnp.mean(x * x, axis=-1, keepdims=True)
    return x * lax.rsqrt(ms + RMS_EPS) * g


def _sigmoid(x):
    return 1.0 / (1.0 + jnp.exp(-x))


def _softplus(z):
    return jnp.maximum(z, 0.0) + jnp.log1p(jnp.exp(-jnp.abs(z)))


def _bdot(x, w):
    return jnp.dot(x.astype(BF16), w, preferred_element_type=F32)


def _split(x, n):
    pieces = []
    for _ in range(n - 1):
        p = x.astype(BF16)
        pieces.append(p)
        x = x - p.astype(F32)
    pieces.append(x.astype(BF16))
    return pieces


def _mm_pieces(xs, ys, dims=_NN):
    n = max(len(xs), len(ys))
    axis = 1 if dims == _TN else 0
    rows = xs[0].shape[axis]
    out = None
    for j, y in enumerate(ys):
        lhs = xs[:n - j]
        if not lhs:
            continue
        stacked = lhs[0] if len(lhs) == 1 else jnp.concatenate(lhs, axis=axis)
        prod = lax.dot_general(stacked, y, dims, preferred_element_type=F32)
        for i in range(len(lhs)):
            part = prod[i * rows:(i + 1) * rows]
            out = part if out is None else out + part
    return out


def _ffn_kernel(x_ref, g_ref, wg_ref, wu_ref, wd_ref, fg_ref, o_ref, act_ref, *, final_norm):
    x = x_ref[...]
    h = _rmsnorm(x, g_ref[...]).astype(BF16)
    for j in range(D_FF // FF_CHUNK):
        cols = slice(j * FF_CHUNK, (j + 1) * FF_CHUNK)
        gate = jnp.dot(h, wg_ref[:, cols], preferred_element_type=F32)
        up = jnp.dot(h, wu_ref[:, cols], preferred_element_type=F32)
        act_ref[:, cols] = (gate * _sigmoid(gate) * up).astype(BF16)
    y = x + 0.5 * jnp.dot(act_ref[...], wd_ref[...], preferred_element_type=F32)
    if final_norm:
        y = _rmsnorm(y, fg_ref[...])
    o_ref[...] = y


def _ffn(x, g, wg, wu, wd, layer, k, final_g=None):
    T = x.shape[0]
    w_in_spec = pl.BlockSpec((None, None, D_MODEL, D_FF), lambda i: (layer, k, 0, 0), pipeline_mode=pl.Buffered(1))
    w_out_spec = pl.BlockSpec((None, None, D_FF, D_MODEL), lambda i: (layer, k, 0, 0), pipeline_mode=pl.Buffered(1))
    fg = g if final_g is None else final_g
    return pl.pallas_call(
        functools.partial(_ffn_kernel, final_norm=final_g is not None),
        grid=(T // TM_FFN,),
        in_specs=[_rows(D_MODEL, TM_FFN), _resident((1, D_MODEL)), w_in_spec, w_in_spec, w_out_spec,
                  _resident((1, D_MODEL))],
        out_specs=_rows(D_MODEL, TM_FFN),
        out_shape=jax.ShapeDtypeStruct((T, D_MODEL), F32),
        scratch_shapes=[pltpu.VMEM((TM_FFN, D_FF), BF16)],
        compiler_params=_params(),
        name="ffn",
    )(x, g, wg, wu, wd, fg)


def _proj_kernel(x_ref, g_ref, w_ref, pa_ref, pb_ref):
    h = _rmsnorm(x_ref[...], g_ref[...]).astype(BF16)
    pa_ref[...] = jnp.dot(h, w_ref[:, :D_IN_A], preferred_element_type=F32)
    pb_ref[...] = jnp.dot(h, w_ref[:, D_IN_A:], preferred_element_type=F32)


def _even_proj(x, g, w_in):
    T = x.shape[0]
    n_b = 3 * D_B
    return pl.pallas_call(
        _proj_kernel,
        grid=(T // TT,),
        in_specs=[_rows(D_MODEL, TT), _resident((1, D_MODEL)), _resident((D_MODEL, D_IN_A + n_b))],
        out_specs=[_rows(D_IN_A, TT), _rows(n_b, TT)],
        out_shape=[jax.ShapeDtypeStruct((T, D_IN_A), F32), jax.ShapeDtypeStruct((T, n_b), F32)],
        compiler_params=_params(),
        name="even_proj",
    )(x, g, w_in)


def _rwkv_chunks(kt_s, rt_s, bt_s, kd_s, be_s, ke_s, v_s, cum_s, y_s, st_ref):
    L = CHUNK
    reps = GROUP_W // L
    row = lax.broadcasted_iota(jnp.int32, (L, GROUP_W), 0)
    col = lax.broadcasted_iota(jnp.int32, (L, GROUP_W), 1) & (HEAD_A - 1)
    r256 = lax.broadcasted_iota(jnp.int32, (GROUP_W, GROUP_W), 0)
    c256 = lax.broadcasted_iota(jnp.int32, (GROUP_W, GROUP_W), 1)
    same_head = (r256 // HEAD_A) == (c256 // HEAD_A)
    eye_ss = jnp.where(row == col, 1.0, 0.0)
    strict = row > col
    incl = row >= col
    level_masks = []
    s = 1
    while s < L:
        level_masks.append(((row ^ col) < 2 * s) & ((row & s) > (col & s)))
        s *= 2

    lane = lax.broadcasted_iota(jnp.int32, (L, LANES), 1)
    half_masks = [jnp.where(lane < HEAD_A, 1.0, 0.0).astype(BF16), jnp.where(lane >= HEAD_A, 1.0, 0.0).astype(BF16)]
    zero_tile = jnp.zeros((L, LANES), BF16)

    def bd(pieces):
        out = []
        for p in pieces:
            blocks = []
            for h in range(reps):
                lt = h * HEAD_A // LANES
                half = p[:, lt * LANES:(lt + 1) * LANES] * half_masks[h % 2]
                blocks.append(jnp.concatenate([half if t == lt else zero_tile for t in range(GROUP_W // LANES)],
                                              axis=1))
            out.append(jnp.concatenate(blocks, axis=0))
        return out

    def per_head(xs, y, n):
        return _mm_pieces(xs, bd(_split(y, n)))

    def fold(full):
        masked = jnp.where(same_head, full, 0.0)
        return sum(masked[h * L:(h + 1) * L] for h in range(reps))

    cgs = [(ci, gi) for ci in range(TT // L) for gi in range(N_GROUPS)]
    sl = {cg: (slice(cg[0] * L, (cg[0] + 1) * L), slice(cg[1] * GROUP_W, (cg[1] + 1) * GROUP_W)) for cg in cgs}
    kt_p, a_ab, a_ak, a_rb, a_rk, v_bd = {}, {}, {}, {}, {}, {}
    for cg in cgs:
        kt_p[cg] = _split(kt_s[sl[cg]], max(N_SCORE, N_APPLY))
        rt_p = _split(rt_s[sl[cg]], N_SCORE)
        lhs_p = [jnp.concatenate([a, b], axis=0) for a, b in zip(kt_p[cg][:N_SCORE], rt_p)]
        s_b = _mm_pieces(lhs_p, bd(_split(bt_s[sl[cg]], N_SCORE)), _NT)
        s_k = _mm_pieces(lhs_p, bd(_split(kd_s[sl[cg]], N_SCORE)), _NT)
        a_ab[cg] = s_b[:L]
        a_ak[cg] = jnp.where(strict, s_k[:L], 0.0)
        a_rb[cg] = jnp.where(incl, s_b[L:], 0.0)
        a_rk[cg] = jnp.where(incl, s_k[L:], 0.0)

    t_inv = {cg: eye_ss - jnp.where(level_masks[0], a_ab[cg], 0.0) for cg in cgs}
    for m in level_masks[1:]:
        t_p = {cg: _split(t_inv[cg], N_INV) for cg in cgs}
        left = {cg: per_head(t_p[cg], jnp.where(m, a_ab[cg], 0.0), N_INV) for cg in cgs}
        t_inv = {cg: t_inv[cg] - _mm_pieces(_split(left[cg], N_INV), bd(t_p[cg])) for cg in cgs}

    t_p = {cg: _split(t_inv[cg], N_APPLY) for cg in cgs}
    for cg in cgs:
        v_bd[cg] = bd(_split(v_s[sl[cg]], N_APPLY))
    w_mat = {cg: -_mm_pieces(t_p[cg], bd(kt_p[cg][:N_APPLY])) for cg in cgs}
    akv = {cg: _mm_pieces(_split(a_ak[cg], N_APPLY), v_bd[cg]) for cg in cgs}
    u0 = {cg: -per_head(t_p[cg], akv[cg], N_APPLY) for cg in cgs}
    arb_p = {cg: _split(a_rb[cg], N_APPLY) for cg in cgs}
    r_hat = {cg: rt_s[sl[cg]] + per_head(arb_p[cg], w_mat[cg], N_APPLY) for cg in cgs}
    y0 = {cg: per_head(arb_p[cg], u0[cg], N_APPLY) + _mm_pieces(_split(a_rk[cg], N_APPLY), v_bd[cg])
          for cg in cgs}

    m_ss, c_ss = {}, {}
    for cg in cgs:
        be_p = _split(be_s[sl[cg]], N_TRANS)
        ci, gi = cg
        p_end = jnp.exp(cum_s[ci * L + L - 1:ci * L + L, sl[cg][1]])
        m_ss[cg] = fold(_mm_pieces(be_p, _split(w_mat[cg], N_TRANS), _TN)) + eye_ss * p_end
        c_ss[cg] = fold(_mm_pieces(be_p, _split(u0[cg], N_TRANS), _TN)
                        + _mm_pieces(_split(ke_s[sl[cg]], N_TRANS), _split(v_s[sl[cg]], N_TRANS), _TN))

    for cg in cgs:
        gi = cg[1]
        st_bd = bd(_split(st_ref[gi], N_STATE))
        both = _mm_pieces([jnp.concatenate([a, b], axis=0)
                           for a, b in zip(_split(r_hat[cg], N_STATE), _split(m_ss[cg], N_STATE))], st_bd)
        y_s[sl[cg]] = both[:L] + y0[cg]
        st_ref[gi] = both[L:] + c_ss[cg]


def _seg_sum(x, seg):
    return jnp.concatenate([_mm_pieces(_split(x[:, gi * GROUP_W:(gi + 1) * GROUP_W], N_SUM), [seg])
                            for gi in range(N_GROUPS)], axis=1)


def _rwkv_kernel(*refs, has_vres):
    (p_ref, mu_ref, w0_ref, wup_ref, a0_ref, aup_ref, gup_ref, kk_ref, ka_ref, rk_ref, lnw_ref, lnb_ref,
     seg_ref) = refs[:13]
    n_in = 13
    if has_vres:
        vf_ref, v0_ref, vdn_ref, vup_ref = refs[13:17]
        n_in = 17
        ya_ref = refs[n_in]
        scratch = refs[n_in + 1:]
    else:
        ya_ref, vout_ref = refs[n_in:n_in + 2]
        scratch = refs[n_in + 2:]
    ext_ref, r_s, k_s, v_s, g_s, kt_s, rt_s, bt_s, kd_s, be_s, ke_s, cum_s, y_s, st_ref = scratch
    L = CHUNK

    @pl.when(pl.program_id(0) == 0)
    def _():
        ext_ref[0:HALO, :] = jnp.zeros((HALO, D_IN_A), F32)
        st_ref[...] = jnp.zeros_like(st_ref)

    p = p_ref[...]
    ext_ref[HALO:HALO + TT, :] = p
    prev = ext_ref[HALO - 1:HALO - 1 + TT, :]
    ext_ref[0:HALO, :] = p[TT - HALO:TT, :]
    p = p + (prev - p) * mu_ref[...]

    r = p[:, 0:D_A]
    k = p[:, D_A:2 * D_A]
    v = p[:, 2 * D_A:3 * D_A]
    x_wa = p[:, 3 * D_A:3 * D_A + LANES]
    xg = p[:, 3 * D_A + LANES:]

    w = -_softplus(-(w0_ref[...] + _bdot(jnp.tanh(x_wa), wup_ref[...]))) - 0.5
    a = _sigmoid(a0_ref[...] + _bdot(x_wa, aup_ref[...]))
    g_s[...] = _bdot(_sigmoid(xg), gup_ref[...])
    if has_vres:
        low = _bdot(v, vdn_ref[...])
        v = v + (vf_ref[...] - v) * _sigmoid(v0_ref[...] + _bdot(low, vup_ref[...]))
    else:
        vout_ref[...] = v
    kn = k * kk_ref[...]
    kn = kn * lax.rsqrt(jnp.maximum(_seg_sum(kn * kn, seg_ref[...]), 1e-24))
    k = k * (1.0 + (a - 1.0) * ka_ref[...])
    wl = -jnp.exp(w)

    ti = lax.broadcasted_iota(jnp.int32, (TT, TT), 0)
    tj = lax.broadcasted_iota(jnp.int32, (TT, TT), 1)
    tri = jnp.where((ti // L) == (tj // L), jnp.where(tj <= ti, 1.0, 0.0), 0.0).astype(BF16)
    cum = _mm_pieces([tri], _split(wl, N_SUM))
    to_end = jnp.concatenate([cum[c * L + L - 1:c * L + L, :] - cum[c * L:(c + 1) * L, :] for c in range(TT // L)],
                             axis=0)

    b = kn * a
    inv_p = jnp.exp(-cum)
    e_end = jnp.exp(to_end)
    r_s[...] = r
    k_s[...] = k
    v_s[...] = v
    cum_s[...] = cum
    kt_s[...] = kn * jnp.exp(cum - wl)
    rt_s[...] = r * jnp.exp(cum)
    bt_s[...] = b * inv_p
    kd_s[...] = k * inv_p
    be_s[...] = b * e_end
    ke_s[...] = k * e_end

    _rwkv_chunks(kt_s, rt_s, bt_s, kd_s, be_s, ke_s, v_s, cum_s, y_s, st_ref)

    y = y_s[...]
    inv_n = 1.0 / HEAD_A
    mean = _seg_sum(y, seg_ref[...]) * inv_n
    d = y - mean
    var = _seg_sum(d * d, seg_ref[...]) * inv_n
    y = d * lax.rsqrt(var + GN_EPS) * lnw_ref[...] + lnb_ref[...]
    y = y + _seg_sum(r_s[...] * k_s[...] * rk_ref[...], seg_ref[...]) * v_s[...]
    ya_ref[...] = y * g_s[...]


def _rwkv(pa, prm, v_first, vres):
    T = pa.shape[0]
    has_vres = vres is not None
    vec = _resident((1, D_A))
    lora = _resident((LANES, D_A))
    in_specs = [_rows(D_IN_A, TT), _resident((1, D_IN_A)), vec, lora, vec, lora, lora, vec, vec, vec, vec, vec,
                _resident((GROUP_W, GROUP_W))]
    args = [pa, prm["mu"], prm["w0"], prm["w_up"], prm["a0"], prm["a_up"], prm["g_up"], prm["k_k"], prm["k_a"],
            prm["r_k"], prm["ln_w"], prm["ln_b"], prm["seg"]]
    out_specs = [_rows(D_A, TT)]
    out_shape = [jax.ShapeDtypeStruct((T, D_A), F32)]
    if has_vres:
        in_specs += [_rows(D_A, TT), vec, _resident((D_A, LANES)), lora]
        args += [v_first, vres["v0"], vres["v_dn"], vres["v_up"]]
    else:
        out_specs.append(_rows(D_A, TT))
        out_shape.append(jax.ShapeDtypeStruct((T, D_A), F32))
    tile = pltpu.VMEM((TT, D_A), F32)
    scratch = [pltpu.VMEM((HALO + TT, D_IN_A), F32)] + [tile] * 12 + [pltpu.VMEM((N_GROUPS, CHUNK, GROUP_W), F32)]
    outs = pl.pallas_call(
        functools.partial(_rwkv_kernel, has_vres=has_vres),
        grid=(T // TT,),
        in_specs=in_specs,
        out_specs=out_specs,
        out_shape=out_shape,
        scratch_shapes=scratch,
        compiler_params=_params(),
        name="rwkv7",
    )(*args)
    return (outs[0], v_first) if has_vres else (outs[0], outs[1])


def _even_out_kernel(x_ref, ya_ref, pb_ref, cw_ref, wo_ref, o_ref, ext_ref):
    @pl.when(pl.program_id(0) == 0)
    def _():
        ext_ref[0:HALO, :] = jnp.zeros((HALO, D_B), F32)

    pb = pb_ref[...]
    ch = pb[:, D_B:2 * D_B] * pb[:, 2 * D_B:]
    ext_ref[HALO:HALO + TT, :] = ch
    conv = (cw_ref[0:1, :] * ext_ref[HALO - 2:HALO - 2 + TT, :]
            + cw_ref[1:2, :] * ext_ref[HALO - 1:HALO - 1 + TT, :]
            + cw_ref[2:3, :] * ch)
    ext_ref[0:HALO, :] = ch[TT - HALO:TT, :]
    yb = pb[:, :D_B] * conv
    mix = _bdot(jnp.concatenate([ya_ref[...], yb], axis=-1), wo_ref[...])
    o_ref[...] = x_ref[...] + mix


def _even_out(x, ya, pb, conv_w, w_out):
    T = x.shape[0]
    return pl.pallas_call(
        _even_out_kernel,
        grid=(T // TT,),
        in_specs=[_rows(D_MODEL, TT), _rows(D_A, TT), _rows(3 * D_B, TT), _resident(conv_w.shape),
                  _resident((D_MODEL, D_MODEL))],
        out_specs=_rows(D_MODEL, TT),
        out_shape=jax.ShapeDtypeStruct((T, D_MODEL), F32),
        scratch_shapes=[pltpu.VMEM((HALO + TT, D_B), F32)],
        compiler_params=_params(),
        name="even_out",
    )(x, ya, pb, conv_w, w_out)


def _gelu_tanh(x):
    return x * (0.5 * (1.0 + jnp.tanh(0.7978845608028654 * (x + 0.044715 * (x * x * x)))))


def _linear_scan(a, b, h0):
    n, c = a.shape
    nb = n // SUBLANES
    a = a.reshape(nb, SUBLANES, c)
    b = b.reshape(nb, SUBLANES, c)
    sub = lax.broadcasted_iota(jnp.int32, (nb, SUBLANES, c), 1)
    step = 1
    while step < SUBLANES:
        valid = sub >= step
        a_prev = jnp.where(valid, pltpu.roll(a, step, 1), 1.0)
        b_prev = jnp.where(valid, pltpu.roll(b, step, 1), 0.0)
        b = a * b_prev + b
        a = a * a_prev
        step *= 2
    carry = h0
    blocks = []
    for j in range(nb):
        hj = a[j] * carry + b[j]
        blocks.append(hj)
        carry = hj[SUBLANES - 1:SUBLANES, :]
    return jnp.concatenate(blocks, axis=0)


def _odd_kernel(x_ref, g_ref, win_ref, cw_ref, cb_ref, wa_ref, ba_ref, wx_ref, bx_ref, lam_ref, dw_ref, ds_ref,
                wo_ref, o_ref, extu_ref, extd_ref, h_ref):
    i = pl.program_id(0)

    @pl.when(i == 0)
    def _():
        extu_ref[0:HALO, :] = jnp.zeros((HALO, D_C), F32)
        extd_ref[0:POOL_HALO, :] = jnp.zeros((POOL_HALO, D_D), F32)
        h_ref[...] = jnp.zeros_like(h_ref)

    x = x_ref[...]
    h = _rmsnorm(x, g_ref[...]).astype(BF16)
    pc = jnp.dot(h, win_ref[:, :2 * D_C], preferred_element_type=F32)
    pd = jnp.dot(h, win_ref[:, 2 * D_C:], preferred_element_type=F32)
    t_glob = i * TT + lax.broadcasted_iota(jnp.int32, (TT, LANES), 0)

    gate = pc[:, :D_C]
    u_in = pc[:, D_C:]
    extu_ref[HALO:HALO + TT, :] = u_in
    u = cb_ref[...] + cw_ref[3:4, :] * u_in
    for j in range(3):
        u = u + cw_ref[j:j + 1, :] * extu_ref[HALO - 3 + j:HALO - 3 + j + TT, :]
    extu_ref[0:HALO, :] = u_in[TT - HALO:TT, :]
    rec = _sigmoid(_bdot(u, wa_ref[...]) + ba_ref[...])
    inp = _sigmoid(_bdot(u, wx_ref[...]) + bx_ref[...])
    log_a = (-LRU_C) * rec * _softplus(-lam_ref[...])
    a = jnp.exp(log_a)
    mult = jnp.sqrt(-jnp.tanh(log_a) * (a * a + 1.0))
    row = lax.broadcasted_iota(jnp.int32, (TT, D_C), 0)
    mult = jnp.where(row + i * TT == 0, 1.0, mult)
    hs = _linear_scan(a, mult * inp * u, h_ref[...])
    h_ref[...] = hs[TT - 1:TT, :]
    yc = _gelu_tanh(gate) * hs

    extd_ref[POOL_HALO:POOL_HALO + TT, :] = pd
    parts = []
    for gi, win in enumerate(POOL_WINDOWS):
        e = extd_ref[:, gi * G_D:(gi + 1) * G_D]
        span = 1
        while span < win:
            e = e + pltpu.roll(e, span, 0)
            span *= 2
        n_avail = jnp.minimum(t_glob + 1, win).astype(F32)
        parts.append(e[POOL_HALO:, :] / n_avail - pd[:, gi * G_D:(gi + 1) * G_D])
    extd_ref[0:POOL_HALO, :] = pd[TT - POOL_HALO:TT, :]
    yd = _bdot(jnp.concatenate(parts, axis=-1), dw_ref[...]) * ds_ref[...]

    mix = _bdot(jnp.concatenate([yc, yd], axis=-1), wo_ref[...])
    o_ref[...] = x + mix


def _odd_mix(x, g, prm):
    T = x.shape[0]
    vec = _resident((1, D_C))
    sq = _resident((D_C, D_C))
    return pl.pallas_call(
        _odd_kernel,
        grid=(T // TT,),
        in_specs=[_rows(D_MODEL, TT), _resident((1, D_MODEL)), _resident((D_MODEL, 2 * D_C + D_D)),
                  _resident(prm["conv_w"].shape), vec, sq, vec, sq, vec, vec, sq, vec,
                  _resident((D_MODEL, D_MODEL))],
        out_specs=_rows(D_MODEL, TT),
        out_shape=jax.ShapeDtypeStruct((T, D_MODEL), F32),
        scratch_shapes=[pltpu.VMEM((HALO + TT, D_C), F32), pltpu.VMEM((POOL_HALO + TT, D_D), F32),
                        pltpu.VMEM((1, D_C), F32)],
        compiler_params=_params(),
        name="odd_mix",
    )(x, g, prm["w_in"], prm["conv_w"], prm["conv_b"], prm["wa"], prm["ba"], prm["wx"], prm["bx"], prm["lam"],
      prm["dw"], prm["ds"], prm["w_out"])


def _block_diag(w):
    H, n, m = w.shape
    return jnp.einsum("hij,hg->higj", w, jnp.eye(H, dtype=w.dtype)).reshape(H * n, H * m)


def _pad_rows(w, before, total):
    return jnp.pad(w, ((before, total - before - w.shape[0]), (0, 0)))


def kernel(x, norm_g, ffn_wg, ffn_wu, ffn_wd, even_w_in, even_w_out, a_mu, a_w0, a_w_up, a_a0, a_a_up, a_g_up,
           a_k_k, a_k_a, a_r_k, a_ln_w, a_ln_b, a_v0, a_v_dn, a_v_up, b_conv_w, odd_w_in, odd_w_out, c_conv_w,
           c_conv_b, c_wa, c_ba, c_wx, c_bx, c_lam, d_w, d_scale, final_g):
    B, T, D = x.shape
    assert (B, D) == (1, D_MODEL) and T % TM_FFN == 0 and T % TT == 0
    xt = x.reshape(T, D)
    wg, wu, wd = ffn_wg.astype(BF16), ffn_wu.astype(BF16), ffn_wd.astype(BF16)
    row = lambda v: v.reshape(1, -1)
    lora_w = a_w_up.shape[1]
    seg = _block_diag(jnp.ones((GROUP_W // HEAD_A, HEAD_A, HEAD_A), BF16))
    v_first = None
    for layer in range(DEPTH):
        xt = _ffn(xt, row(norm_g[layer, 0]), wg, wu, wd, layer, 0)
        g_mix = row(norm_g[layer, 1])
        if layer % 2 == 0:
            e = layer // 2
            pa, pb = _even_proj(xt, g_mix, even_w_in[e].astype(BF16))
            prm = dict(
                mu=row(a_mu[e]), w0=row(a_w0[e]), a0=row(a_a0[e]),
                w_up=_pad_rows(a_w_up[e], 0, LANES).astype(BF16),
                a_up=_pad_rows(a_a_up[e], lora_w, LANES).astype(BF16),
                g_up=a_g_up[e].astype(BF16),
                k_k=row(a_k_k[e]), k_a=row(a_k_a[e]), r_k=row(a_r_k[e]), ln_w=row(a_ln_w[e]), ln_b=row(a_ln_b[e]),
                seg=seg)
            vres = None
            if e > 0:
                vres = dict(v0=row(a_v0[e - 1]),
                            v_dn=jnp.pad(a_v_dn[e - 1], ((0, 0), (0, LANES - LORA_V))).astype(BF16),
                            v_up=_pad_rows(a_v_up[e - 1], 0, LANES).astype(BF16))
            ya, v_first = _rwkv(pa, prm, v_first, vres)
            xt = _even_out(xt, ya, pb, b_conv_w[e], even_w_out[e].astype(BF16))
        else:
            o = layer // 2
            prm = dict(
                w_in=odd_w_in[o].astype(BF16), w_out=odd_w_out[o].astype(BF16),
                conv_w=c_conv_w[o], conv_b=row(c_conv_b[o]),
                wa=_block_diag(c_wa[o]).astype(BF16), ba=row(c_ba[o]),
                wx=_block_diag(c_wx[o]).astype(BF16), bx=row(c_bx[o]), lam=row(c_lam[o]),
                dw=_block_diag(d_w[o]).astype(BF16), ds=row(d_scale[o]))
            xt = _odd_mix(xt, g_mix, prm)
        last = layer == DEPTH - 1
        xt = _ffn(xt, row(norm_g[layer, 2]), wg, wu, wd, layer, 1, final_g=row(final_g) if last else None)
    return xt.reshape(B, T, D)
```

```python
import functools

import jax
import jax.numpy as jnp
from jax import lax
from jax.experimental import pallas as pl
from jax.experimental.pallas import tpu as pltpu

F32 = jnp.float32
BF16 = jnp.bfloat16

D_MODEL = 1024
DEPTH = 4
D_A = 512
HEAD_A = 64
LORA_V = 32
D_IN_A = 1792
D_B = 512
D_C = 512
H_C = 8
D_D = 512
POOL_WINDOWS = (2, 4, 8, 16)
G_D = 128
D_FF = 2816
GN_EPS = 64e-5
RMS_EPS = 1e-6
LRU_C = 8.0

LANES = 128
SUBLANES = 8
MXU_DIM = 256
VMEM_LIMIT_BYTES = 56 * 1024 * 1024

TM_FFN = 1024
FF_CHUNK = MXU_DIM
TT = 256
CHUNK = 64
GROUP_W = MXU_DIM
N_GROUPS = D_A // GROUP_W
HALO = SUBLANES
POOL_HALO = 2 * SUBLANES

N_SCORE = 1
N_INV = 1
N_APPLY = 1
N_TRANS = 1
N_STATE = 2
N_SUM = 2

_NN = (((1,), (0,)), ((), ()))
_NT = (((1,), (1,)), ((), ()))
_TN = (((0,), (0,)), ((), ()))


def _params(n_axes=1):
    return pltpu.CompilerParams(dimension_semantics=("arbitrary",) * n_axes,
                                vmem_limit_bytes=VMEM_LIMIT_BYTES)


def _resident(shape):
    zeros = (0,) * len(shape)
    return pl.BlockSpec(shape, lambda i: zeros, pipeline_mode=pl.Buffered(1))


def _rows(width, tile):
    return pl.BlockSpec((tile, width), lambda i: (i, 0))


def _rmsnorm(x, g):
    ms = jnp.mean(x * x, axis=-1, keepdims=True)
    return x * lax.rsqrt(ms + RMS_EPS) * g


def _sigmoid(x):
    return 1.0 / (1.0 + jnp.exp(-x))


def _softplus(z):
    return jnp.maximum(z, 0.0) + jnp.log1p(jnp.exp(-jnp.abs(z)))


def _bdot(x, w):
    return jnp.dot(x.astype(BF16), w, preferred_element_type=F32)


def _split(x, n):
    pieces = []
    for _ in range(n - 1):
        p = x.astype(BF16)
        pieces.append(p)
        x = x - p.astype(F32)
    pieces.append(x.astype(BF16))
    return pieces


def _mm_pieces(xs, ys, dims=_NN):
    n = max(len(xs), len(ys))
    axis = 1 if dims == _TN else 0
    rows = xs[0].shape[axis]
    out = None
    for j, y in enumerate(ys):
        lhs = xs[:n - j]
        if not lhs:
            continue
        stacked = lhs[0] if len(lhs) == 1 else jnp.concatenate(lhs, axis=axis)
        prod = lax.dot_general(stacked, y, dims, preferred_element_type=F32)
        for i in range(len(lhs)):
            part = prod[i * rows:(i + 1) * rows]
            out = part if out is None else out + part
    return out


def _ffn_kernel(x_ref, g_ref, wg_ref, wu_ref, wd_ref, fg_ref, o_ref, act_ref, *, final_norm):
    x = x_ref[...]
    h = _rmsnorm(x, g_ref[...]).astype(BF16)
    for j in range(D_FF // FF_CHUNK):
        cols = slice(j * FF_CHUNK, (j + 1) * FF_CHUNK)
        gate = jnp.dot(h, wg_ref[:, cols], preferred_element_type=F32)
        up = jnp.dot(h, wu_ref[:, cols], preferred_element_type=F32)
        act_ref[:, cols] = (gate * _sigmoid(gate) * up).astype(BF16)
    y = x + 0.5 * jnp.dot(act_ref[...], wd_ref[...], preferred_element_type=F32)
    if final_norm:
        y = _rmsnorm(y, fg_ref[...])
    o_ref[...] = y


def _ffn(x, g, wg, wu, wd, layer, k, final_g=None):
    T = x.shape[0]
    w_in_spec = pl.BlockSpec((None, None, D_MODEL, D_FF), lambda i: (layer, k, 0, 0), pipeline_mode=pl.Buffered(1))
    w_out_spec = pl.BlockSpec((None, None, D_FF, D_MODEL), lambda i: (layer, k, 0, 0), pipeline_mode=pl.Buffered(1))
    fg = g if final_g is None else final_g
    return pl.pallas_call(
        functools.partial(_ffn_kernel, final_norm=final_g is not None),
        grid=(T // TM_FFN,),
        in_specs=[_rows(D_MODEL, TM_FFN), _resident((1, D_MODEL)), w_in_spec, w_in_spec, w_out_spec,
                  _resident((1, D_MODEL))],
        out_specs=_rows(D_MODEL, TM_FFN),
        out_shape=jax.ShapeDtypeStruct((T, D_MODEL), F32),
        scratch_shapes=[pltpu.VMEM((TM_FFN, D_FF), BF16)],
        compiler_params=_params(),
        name="ffn",
    )(x, g, wg, wu, wd, fg)


def _rwkv_chunks(kt_s, rt_s, bt_s, kd_s, be_s, ke_s, v_s, cum_s, y_s, st_ref):
    L = CHUNK
    reps = GROUP_W // L
    row = lax.broadcasted_iota(jnp.int32, (L, GROUP_W), 0)
    col = lax.broadcasted_iota(jnp.int32, (L, GROUP_W), 1) & (HEAD_A - 1)
    r256 = lax.broadcasted_iota(jnp.int32, (GROUP_W, GROUP_W), 0)
    c256 = lax.broadcasted_iota(jnp.int32, (GROUP_W, GROUP_W), 1)
    same_head = (r256 // HEAD_A) == (c256 // HEAD_A)
    eye_ss = jnp.where(row == col, 1.0, 0.0)
    strict = row > col
    incl = row >= col
    level_masks = []
    s = 1
    while s < L:
        level_masks.append(((row ^ col) < 2 * s) & ((row & s) > (col & s)))
        s *= 2

    lane = lax.broadcasted_iota(jnp.int32, (L, LANES), 1)
    half_masks = [jnp.where(lane < HEAD_A, 1.0, 0.0).astype(BF16), jnp.where(lane >= HEAD_A, 1.0, 0.0).astype(BF16)]
    zero_tile = jnp.zeros((L, LANES), BF16)

    def bd(pieces):
        out = []
        for p in pieces:
            blocks = []
            for h in range(reps):
                lt = h * HEAD_A // LANES
                half = p[:, lt * LANES:(lt + 1) * LANES] * half_masks[h % 2]
                blocks.append(jnp.concatenate([half if t == lt else zero_tile for t in range(GROUP_W // LANES)],
                                              axis=1))
            out.append(jnp.concatenate(blocks, axis=0))
        return out

    def per_head(xs, y, n):
        return _mm_pieces(xs, bd(_split(y, n)))

    def fold(full):
        masked = jnp.where(same_head, full, 0.0)
        return sum(masked[h * L:(h + 1) * L] for h in range(reps))

    cgs = [(ci, gi) for ci in range(TT // L) for gi in range(N_GROUPS)]
    sl = {cg: (slice(cg[0] * L, (cg[0] + 1) * L), slice(cg[1] * GROUP_W, (cg[1] + 1) * GROUP_W)) for cg in cgs}
    kt_p, a_ab, a_ak, a_rb, a_rk, v_bd = {}, {}, {}, {}, {}, {}
    for cg in cgs:
        kt_p[cg] = _split(kt_s[sl[cg]], max(N_SCORE, N_APPLY))
        rt_p = _split(rt_s[sl[cg]], N_SCORE)
        lhs_p = [jnp.concatenate([a, b], axis=0) for a, b in zip(kt_p[cg][:N_SCORE], rt_p)]
        s_b = _mm_pieces(lhs_p, bd(_split(bt_s[sl[cg]], N_SCORE)), _NT)
        s_k = _mm_pieces(lhs_p, bd(_split(kd_s[sl[cg]], N_SCORE)), _NT)
        a_ab[cg] = s_b[:L]
        a_ak[cg] = jnp.where(strict, s_k[:L], 0.0)
        a_rb[cg] = jnp.where(incl, s_b[L:], 0.0)
        a_rk[cg] = jnp.where(incl, s_k[L:], 0.0)

    t_inv = {cg: eye_ss - jnp.where(level_masks[0], a_ab[cg], 0.0) for cg in cgs}
    for m in level_masks[1:]:
        t_p = {cg: _split(t_inv[cg], N_INV) for cg in cgs}
        left = {cg: per_head(t_p[cg], jnp.where(m, a_ab[cg], 0.0), N_INV) for cg in cgs}
        t_inv = {cg: t_inv[cg] - _mm_pieces(_split(left[cg], N_INV), bd(t_p[cg])) for cg in cgs}

    t_p = {cg: _split(t_inv[cg], N_APPLY) for cg in cgs}
    for cg in cgs:
        v_bd[cg] = bd(_split(v_s[sl[cg]], N_APPLY))
    w_mat = {cg: -_mm_pieces(t_p[cg], bd(kt_p[cg][:N_APPLY])) for cg in cgs}
    akv = {cg: _mm_pieces(_split(a_ak[cg], N_APPLY), v_bd[cg]) for cg in cgs}
    u0 = {cg: -per_head(t_p[cg], akv[cg], N_APPLY) for cg in cgs}
    arb_p = {cg: _split(a_rb[cg], N_APPLY) for cg in cgs}
    r_hat = {cg: rt_s[sl[cg]] + per_head(arb_p[cg], w_mat[cg], N_APPLY) for cg in cgs}
    y0 = {cg: per_head(arb_p[cg], u0[cg], N_APPLY) + _mm_pieces(_split(a_rk[cg], N_APPLY), v_bd[cg])
          for cg in cgs}

    m_ss, c_ss = {}, {}
    for cg in cgs:
        be_p = _split(be_s[sl[cg]], N_TRANS)
        ci, gi = cg
        p_end = jnp.exp(cum_s[ci * L + L - 1:ci * L + L, sl[cg][1]])
        m_ss[cg] = fold(_mm_pieces(be_p, _split(w_mat[cg], N_TRANS), _TN)) + eye_ss * p_end
        c_ss[cg] = fold(_mm_pieces(be_p, _split(u0[cg], N_TRANS), _TN)
                        + _mm_pieces(_split(ke_s[sl[cg]], N_TRANS), _split(v_s[sl[cg]], N_TRANS), _TN))

    for cg in cgs:
        gi = cg[1]
        st_bd = bd(_split(st_ref[gi], N_STATE))
        both = _mm_pieces([jnp.concatenate([a, b], axis=0)
                           for a, b in zip(_split(r_hat[cg], N_STATE), _split(m_ss[cg], N_STATE))], st_bd)
        y_s[sl[cg]] = both[:L] + y0[cg]
        st_ref[gi] = both[L:] + c_ss[cg]


def _seg_sum(x, seg):
    return jnp.concatenate([_mm_pieces(_split(x[:, gi * GROUP_W:(gi + 1) * GROUP_W], N_SUM), [seg])
                            for gi in range(N_GROUPS)], axis=1)


def _even_kernel(*refs, has_vres):
    (xc_ref, xp_ref, gn_ref, win_ref, mu_ref, w0_ref, wup_ref, a0_ref, aup_ref, gup_ref, kk_ref, ka_ref, rk_ref,
     lnw_ref, lnb_ref, seg_ref, cw_ref, wo_ref) = refs[:18]
    n_in = 18
    if has_vres:
        vf_ref, v0_ref, vdn_ref, vup_ref = refs[18:22]
        n_in = 22
        o_ref = refs[n_in]
        scratch = refs[n_in + 1:]
    else:
        o_ref, vout_ref = refs[n_in:n_in + 2]
        scratch = refs[n_in + 2:]
    (ext_ref, extc_ref, bonus_s, g_s, yb_s, v_s, kt_s, rt_s, bt_s, kd_s, be_s, ke_s, cum_s, y_s, st_ref) = scratch
    L = CHUNK
    i = pl.program_id(0)
    n_tiles = pl.num_programs(0) - 1

    @pl.when(i == 0)
    def _():
        ext_ref[0:HALO, :] = jnp.zeros((HALO, D_IN_A), F32)
        extc_ref[0:HALO, :] = jnp.zeros((HALO, D_B), F32)
        st_ref[...] = jnp.zeros_like(st_ref)
        for ref in (bonus_s, g_s, yb_s, v_s, kt_s, rt_s, bt_s, kd_s, be_s, ke_s, cum_s):
            ref[...] = jnp.zeros_like(ref)

    _rwkv_chunks(kt_s, rt_s, bt_s, kd_s, be_s, ke_s, v_s, cum_s, y_s, st_ref)
    y = y_s[...]
    inv_n = 1.0 / HEAD_A
    mean = _seg_sum(y, seg_ref[...]) * inv_n
    d = y - mean
    var = _seg_sum(d * d, seg_ref[...]) * inv_n
    y = d * lax.rsqrt(var + GN_EPS) * lnw_ref[...] + lnb_ref[...]
    ya = (y + bonus_s[...]) * g_s[...]
    mix = _bdot(jnp.concatenate([ya, yb_s[...]], axis=-1), wo_ref[...])
    o_ref[...] = xp_ref[...] + mix

    h = _rmsnorm(xc_ref[...], gn_ref[...]).astype(BF16)
    p = jnp.dot(h, win_ref[:, :D_IN_A], preferred_element_type=F32)
    pb = jnp.dot(h, win_ref[:, D_IN_A:], preferred_element_type=F32)

    ext_ref[HALO:HALO + TT, :] = p
    prev = ext_ref[HALO - 1:HALO - 1 + TT, :]
    ext_ref[0:HALO, :] = p[TT - HALO:TT, :]
    p = p + (prev - p) * mu_ref[...]

    r = p[:, 0:D_A]
    k = p[:, D_A:2 * D_A]
    v = p[:, 2 * D_A:3 * D_A]
    x_wa = p[:, 3 * D_A:3 * D_A + LANES]
    xg = p[:, 3 * D_A + LANES:]

    w = -_softplus(-(w0_ref[...] + _bdot(jnp.tanh(x_wa), wup_ref[...]))) - 0.5
    a = _sigmoid(a0_ref[...] + _bdot(x_wa, aup_ref[...]))
    g = _bdot(_sigmoid(xg), gup_ref[...])
    if has_vres:
        low = _bdot(v, vdn_ref[...])
        v = v + (vf_ref[...] - v) * _sigmoid(v0_ref[...] + _bdot(low, vup_ref[...]))
    kn = k * kk_ref[...]
    kn = kn * lax.rsqrt(jnp.maximum(_seg_sum(kn * kn, seg_ref[...]), 1e-24))
    k = k * (1.0 + (a - 1.0) * ka_ref[...])
    wl = -jnp.exp(w)

    ti = lax.broadcasted_iota(jnp.int32, (TT, TT), 0)
    tj = lax.broadcasted_iota(jnp.int32, (TT, TT), 1)
    tri = jnp.where((ti // L) == (tj // L), jnp.where(tj <= ti, 1.0, 0.0), 0.0).astype(BF16)
    cum = _mm_pieces([tri], _split(wl, N_SUM))
    to_end = jnp.concatenate([cum[c * L + L - 1:c * L + L, :] - cum[c * L:(c + 1) * L, :] for c in range(TT // L)],
                             axis=0)

    b = kn * a
    inv_p = jnp.exp(-cum)
    e_end = jnp.exp(to_end)

    ch = pb[:, D_B:2 * D_B] * pb[:, 2 * D_B:]
    extc_ref[HALO:HALO + TT, :] = ch
    conv = (cw_ref[0:1, :] * extc_ref[HALO - 2:HALO - 2 + TT, :]
            + cw_ref[1:2, :] * extc_ref[HALO - 1:HALO - 1 + TT, :]
            + cw_ref[2:3, :] * ch)
    extc_ref[0:HALO, :] = ch[TT - HALO:TT, :]

    bonus_s[...] = _seg_sum(r * k * rk_ref[...], seg_ref[...]) * v
    g_s[...] = g
    yb_s[...] = pb[:, :D_B] * conv
    v_s[...] = v
    cum_s[...] = cum
    kt_s[...] = kn * jnp.exp(cum - wl)
    rt_s[...] = r * jnp.exp(cum)
    bt_s[...] = b * inv_p
    kd_s[...] = k * inv_p
    be_s[...] = b * e_end
    ke_s[...] = k * e_end
    if not has_vres:
        @pl.when(i < n_tiles)
        def _():
            vout_ref[...] = v


def _even_mix(x, g, prm, v_first, vres):
    T = x.shape[0]
    n_tiles = T // TT
    has_vres = vres is not None
    cur = lambda w: pl.BlockSpec((TT, w), lambda i: (jnp.minimum(i, n_tiles - 1), 0))
    prv = lambda w: pl.BlockSpec((TT, w), lambda i: (jnp.maximum(i - 1, 0), 0))
    vec = _resident((1, D_A))
    lora = _resident((LANES, D_A))
    in_specs = [cur(D_MODEL), prv(D_MODEL), _resident((1, D_MODEL)), _resident((D_MODEL, D_IN_A + 3 * D_B)),
                _resident((1, D_IN_A)), vec, lora, vec, lora, lora, vec, vec, vec, vec, vec,
                _resident((GROUP_W, GROUP_W)), _resident(prm["conv_w"].shape), _resident((D_MODEL, D_MODEL))]
    args = [x, x, g, prm["w_in"], prm["mu"], prm["w0"], prm["w_up"], prm["a0"], prm["a_up"], prm["g_up"], prm["k_k"],
            prm["k_a"], prm["r_k"], prm["ln_w"], prm["ln_b"], prm["seg"], prm["conv_w"], prm["w_out"]]
    out_specs = [prv(D_MODEL)]
    out_shape = [jax.ShapeDtypeStruct((T, D_MODEL), F32)]
    if has_vres:
        in_specs += [cur(D_A), vec, _resident((D_A, LANES)), lora]
        args += [v_first, vres["v0"], vres["v_dn"], vres["v_up"]]
    else:
        out_specs.append(cur(D_A))
        out_shape.append(jax.ShapeDtypeStruct((T, D_A), F32))
    tile = pltpu.VMEM((TT, D_A), F32)
    scratch = ([pltpu.VMEM((HALO + TT, D_IN_A), F32), pltpu.VMEM((HALO + TT, D_B), F32)] + [tile] * 12
               + [pltpu.VMEM((N_GROUPS, CHUNK, GROUP_W), F32)])
    outs = pl.pallas_call(
        functools.partial(_even_kernel, has_vres=has_vres),
        grid=(n_tiles + 1,),
        in_specs=in_specs,
        out_specs=out_specs,
        out_shape=out_shape,
        scratch_shapes=scratch,
        compiler_params=_params(),
        name="even_mix",
    )(*args)
    return (outs[0], v_first) if has_vres else (outs[0], outs[1])


def _gelu_tanh(x):
    return x * (0.5 * (1.0 + jnp.tanh(0.7978845608028654 * (x + 0.044715 * (x * x * x)))))


def _linear_scan(a, b, h0):
    n, c = a.shape
    nb = n // SUBLANES
    a = a.reshape(nb, SUBLANES, c)
    b = b.reshape(nb, SUBLANES, c)
    sub = lax.broadcasted_iota(jnp.int32, (nb, SUBLANES, c), 1)
    step = 1
    while step < SUBLANES:
        valid = sub >= step
        a_prev = jnp.where(valid, pltpu.roll(a, step, 1), 1.0)
        b_prev = jnp.where(valid, pltpu.roll(b, step, 1), 0.0)
        b = a * b_prev + b
        a = a * a_prev
        step *= 2
    carry = h0
    blocks = []
    for j in range(nb):
        hj = a[j] * carry + b[j]
        blocks.append(hj)
        carry = hj[SUBLANES - 1:SUBLANES, :]
    return jnp.concatenate(blocks, axis=0)


def _odd_kernel(x_ref, g_ref, win_ref, cw_ref, cb_ref, wa_ref, ba_ref, wx_ref, bx_ref, lam_ref, dw_ref, ds_ref,
                wo_ref, o_ref, extu_ref, extd_ref, h_ref):
    i = pl.program_id(0)

    @pl.when(i == 0)
    def _():
        extu_ref[0:HALO, :] = jnp.zeros((HALO, D_C), F32)
        extd_ref[0:POOL_HALO, :] = jnp.zeros((POOL_HALO, D_D), F32)
        h_ref[...] = jnp.zeros_like(h_ref)

    x = x_ref[...]
    h = _rmsnorm(x, g_ref[...]).astype(BF16)
    pc = jnp.dot(h, win_ref[:, :2 * D_C], preferred_element_type=F32)
    pd = jnp.dot(h, win_ref[:, 2 * D_C:], preferred_element_type=F32)
    t_glob = i * TT + lax.broadcasted_iota(jnp.int32, (TT, LANES), 0)

    gate = pc[:, :D_C]
    u_in = pc[:, D_C:]
    extu_ref[HALO:HALO + TT, :] = u_in
    u = cb_ref[...] + cw_ref[3:4, :] * u_in
    for j in range(3):
        u = u + cw_ref[j:j + 1, :] * extu_ref[HALO - 3 + j:HALO - 3 + j + TT, :]
    extu_ref[0:HALO, :] = u_in[TT - HALO:TT, :]
    rec = _sigmoid(_bdot(u, wa_ref[...]) + ba_ref[...])
    inp = _sigmoid(_bdot(u, wx_ref[...]) + bx_ref[...])
    log_a = (-LRU_C) * rec * _softplus(-lam_ref[...])
    a = jnp.exp(log_a)
    mult = jnp.sqrt(-jnp.tanh(log_a) * (a * a + 1.0))
    row = lax.broadcasted_iota(jnp.int32, (TT, D_C), 0)
    mult = jnp.where(row + i * TT == 0, 1.0, mult)
    hs = _linear_scan(a, mult * inp * u, h_ref[...])
    h_ref[...] = hs[TT - 1:TT, :]
    yc = _gelu_tanh(gate) * hs

    extd_ref[POOL_HALO:POOL_HALO + TT, :] = pd
    parts = []
    for gi, win in enumerate(POOL_WINDOWS):
        e = extd_ref[:, gi * G_D:(gi + 1) * G_D]
        span = 1
        while span < win:
            e = e + pltpu.roll(e, span, 0)
            span *= 2
        n_avail = jnp.minimum(t_glob + 1, win).astype(F32)
        parts.append(e[POOL_HALO:, :] / n_avail - pd[:, gi * G_D:(gi + 1) * G_D])
    extd_ref[0:POOL_HALO, :] = pd[TT - POOL_HALO:TT, :]
    yd = _bdot(jnp.concatenate(parts, axis=-1), dw_ref[...]) * ds_ref[...]

    mix = _bdot(jnp.concatenate([yc, yd], axis=-1), wo_ref[...])
    o_ref[...] = x + mix


def _odd_mix(x, g, prm):
    T = x.shape[0]
    vec = _resident((1, D_C))
    sq = _resident((D_C, D_C))
    return pl.pallas_call(
        _odd_kernel,
        grid=(T // TT,),
        in_specs=[_rows(D_MODEL, TT), _resident((1, D_MODEL)), _resident((D_MODEL, 2 * D_C + D_D)),
                  _resident(prm["conv_w"].shape), vec, sq, vec, sq, vec, vec, sq, vec,
                  _resident((D_MODEL, D_MODEL))],
        out_specs=_rows(D_MODEL, TT),
        out_shape=jax.ShapeDtypeStruct((T, D_MODEL), F32),
        scratch_shapes=[pltpu.VMEM((HALO + TT, D_C), F32), pltpu.VMEM((POOL_HALO + TT, D_D), F32),
                        pltpu.VMEM((1, D_C), F32)],
        compiler_params=_params(),
        name="odd_mix",
    )(x, g, prm["w_in"], prm["conv_w"], prm["conv_b"], prm["wa"], prm["ba"], prm["wx"], prm["bx"], prm["lam"],
      prm["dw"], prm["ds"], prm["w_out"])


def _block_diag(w):
    H, n, m = w.shape
    return jnp.einsum("hij,hg->higj", w, jnp.eye(H, dtype=w.dtype)).reshape(H * n, H * m)


def _pad_rows(w, before, total):
    return jnp.pad(w, ((before, total - before - w.shape[0]), (0, 0)))


def kernel(x, norm_g, ffn_wg, ffn_wu, ffn_wd, even_w_in, even_w_out, a_mu, a_w0, a_w_up, a_a0, a_a_up, a_g_up,
           a_k_k, a_k_a, a_r_k, a_ln_w, a_ln_b, a_v0, a_v_dn, a_v_up, b_conv_w, odd_w_in, odd_w_out, c_conv_w,
           c_conv_b, c_wa, c_ba, c_wx, c_bx, c_lam, d_w, d_scale, final_g):
    B, T, D = x.shape
    assert (B, D) == (1, D_MODEL) and T % TM_FFN == 0 and T % TT == 0
    xt = x.reshape(T, D)
    wg, wu, wd = ffn_wg.astype(BF16), ffn_wu.astype(BF16), ffn_wd.astype(BF16)
    row = lambda v: v.reshape(1, -1)
    lora_w = a_w_up.shape[1]
    seg = _block_diag(jnp.ones((GROUP_W // HEAD_A, HEAD_A, HEAD_A), BF16))
    v_first = None
    for layer in range(DEPTH):
        xt = _ffn(xt, row(norm_g[layer, 0]), wg, wu, wd, layer, 0)
        g_mix = row(norm_g[layer, 1])
        if layer % 2 == 0:
            e = layer // 2
            prm = dict(
                w_in=even_w_in[e].astype(BF16), w_out=even_w_out[e].astype(BF16), conv_w=b_conv_w[e],
                mu=row(a_mu[e]), w0=row(a_w0[e]), a0=row(a_a0[e]),
                w_up=_pad_rows(a_w_up[e], 0, LANES).astype(BF16),
                a_up=_pad_rows(a_a_up[e], lora_w, LANES).astype(BF16),
                g_up=a_g_up[e].astype(BF16),
                k_k=row(a_k_k[e]), k_a=row(a_k_a[e]), r_k=row(a_r_k[e]), ln_w=row(a_ln_w[e]), ln_b=row(a_ln_b[e]),
                seg=seg)
            vres = None
            if e > 0:
                vres = dict(v0=row(a_v0[e - 1]),
                            v_dn=jnp.pad(a_v_dn[e - 1], ((0, 0), (0, LANES - LORA_V))).astype(BF16),
                            v_up=_pad_rows(a_v_up[e - 1], 0, LANES).astype(BF16))
            xt, v_first = _even_mix(xt, g_mix, prm, v_first, vres)
        else:
            o = layer // 2
            prm = dict(
                w_in=odd_w_in[o].astype(BF16), w_out=odd_w_out[o].astype(BF16),
                conv_w=c_conv_w[o], conv_b=row(c_conv_b[o]),
                wa=_block_diag(c_wa[o]).astype(BF16), ba=row(c_ba[o]),
                wx=_block_diag(c_wx[o]).astype(BF16), bx=row(c_bx[o]), lam=row(c_lam[o]),
                dw=_block_diag(d_w[o]).astype(BF16), ds=row(d_scale[o]))
            xt = _odd_mix(xt, g_mix, prm)
        last = layer == DEPTH - 1
        xt = _ffn(xt, row(norm_g[layer, 2]), wg, wu, wd, layer, 1, final_g=row(final_g) if last else None)
    return xt.reshape(B, T, D)
```

```python
import functools

import jax
import jax.numpy as jnp
from jax import lax
from jax.experimental import pallas as pl
from jax.experimental.pallas import tpu as pltpu

F32 = jnp.float32
BF16 = jnp.bfloat16

D_MODEL = 1024
DEPTH = 4
D_A = 512
HEAD_A = 64
LORA_V = 32
D_IN_A = 1792
D_B = 512
D_C = 512
H_C = 8
D_D = 512
POOL_WINDOWS = (2, 4, 8, 16)
G_D = 128
D_FF = 2816
GN_EPS = 64e-5
RMS_EPS = 1e-6
LRU_C = 8.0
DECAY_SCALE = 0.6065306597126334

LANES = 128
SUBLANES = 8
MXU_DIM = 256
VMEM_LIMIT_BYTES = 56 * 1024 * 1024

TM_FFN = 1024
FF_CHUNK = MXU_DIM
TT = 256
CHUNK = 64
GROUP_W = MXU_DIM
N_GROUPS = D_A // GROUP_W
HALO = SUBLANES
POOL_HALO = 2 * SUBLANES

N_SCORE = 1
N_INV = 1
N_APPLY = 1
N_TRANS = 1
N_STATE = 2
N_SUM = 2

_NN = (((1,), (0,)), ((), ()))
_NT = (((1,), (1,)), ((), ()))
_TN = (((0,), (0,)), ((), ()))


def _params(n_axes=1):
    return pltpu.CompilerParams(dimension_semantics=("arbitrary",) * n_axes,
                                vmem_limit_bytes=VMEM_LIMIT_BYTES)


def _resident(shape):
    zeros = (0,) * len(shape)
    return pl.BlockSpec(shape, lambda i: zeros, pipeline_mode=pl.Buffered(1))


def _rows(width, tile):
    return pl.BlockSpec((tile, width), lambda i: (i, 0))


def _rmsnorm(x, g):
    ms = jnp.mean(x * x, axis=-1, keepdims=True)
    return x * lax.rsqrt(ms + RMS_EPS) * g


def _sigmoid(x):
    return 1.0 / (1.0 + jnp.exp(-x))


def _softplus(z):
    return jnp.maximum(z, 0.0) + jnp.log1p(jnp.exp(-jnp.abs(z)))


def _bdot(x, w):
    return jnp.dot(x.astype(BF16), w, preferred_element_type=F32)


def _split(x, n):
    pieces = []
    for _ in range(n - 1):
        p = x.astype(BF16)
        pieces.append(p)
        x = x - p.astype(F32)
    pieces.append(x.astype(BF16))
    return pieces


def _mm_pieces(xs, ys, dims=_NN):
    n = max(len(xs), len(ys))
    axis = 1 if dims == _TN else 0
    rows = xs[0].shape[axis]
    out = None
    for j, y in enumerate(ys):
        lhs = xs[:n - j]
        if not lhs:
            continue
        stacked = lhs[0] if len(lhs) == 1 else jnp.concatenate(lhs, axis=axis)
        prod = lax.dot_general(stacked, y, dims, preferred_element_type=F32)
        for i in range(len(lhs)):
            part = prod[i * rows:(i + 1) * rows]
            out = part if out is None else out + part
    return out


def _ffn_kernel(*refs, final_norm, cast_next):
    x_ref, g_ref, wg_ref, wu_ref, wd_ref, fg_ref = refs[:6]
    if cast_next:
        o_ref = refs[9]
        for src, dst in zip(refs[6:9], refs[10:13]):
            dst[...] = src[...].astype(BF16)
    else:
        o_ref = refs[6]
    act_ref = refs[-1]
    x = x_ref[...]
    h = _rmsnorm(x, g_ref[...]).astype(BF16)
    for j in range(D_FF // FF_CHUNK):
        cols = slice(j * FF_CHUNK, (j + 1) * FF_CHUNK)
        gate = jnp.dot(h, wg_ref[:, cols], preferred_element_type=F32)
        up = jnp.dot(h, wu_ref[:, cols], preferred_element_type=F32)
        act_ref[:, cols] = (gate * _sigmoid(gate) * up).astype(BF16)
    y = x + 0.5 * jnp.dot(act_ref[...], wd_ref[...], preferred_element_type=F32)
    if final_norm:
        y = _rmsnorm(y, fg_ref[...])
    o_ref[...] = y


def _ffn(x, g, weights, next_f32=None, final_g=None):
    T = x.shape[0]
    n_steps = T // TM_FFN
    fg = g if final_g is None else final_g
    in_specs = [_rows(D_MODEL, TM_FFN), _resident((1, D_MODEL)), _resident((D_MODEL, D_FF)),
                _resident((D_MODEL, D_FF)), _resident((D_FF, D_MODEL)), _resident((1, D_MODEL))]
    args = [x, g, *weights, fg]
    out_specs = [_rows(D_MODEL, TM_FFN)]
    out_shape = [jax.ShapeDtypeStruct((T, D_MODEL), F32)]
    if next_f32 is not None:
        *w_next, layer, k = next_f32
        for w in w_next:
            rows, cols = w.shape[2] // n_steps, w.shape[3]
            in_specs.append(pl.BlockSpec((None, None, rows, cols), lambda i: (layer, k, i, 0)))
            out_specs.append(pl.BlockSpec((rows, cols), lambda i: (i, 0)))
            out_shape.append(jax.ShapeDtypeStruct(w.shape[2:], BF16))
        args += w_next
    outs = pl.pallas_call(
        functools.partial(_ffn_kernel, final_norm=final_g is not None, cast_next=next_f32 is not None),
        grid=(n_steps,),
        in_specs=in_specs,
        out_specs=out_specs,
        out_shape=out_shape,
        scratch_shapes=[pltpu.VMEM((TM_FFN, D_FF), BF16)],
        compiler_params=_params(),
        name="ffn",
    )(*args)
    return outs[0], tuple(outs[1:])


def _rwkv_chunks(kt_s, rt_s, bt_s, kd_s, be_s, ke_s, v_s, cum_s, y_s, st_ref):
    L = CHUNK
    reps = GROUP_W // L
    row = lax.broadcasted_iota(jnp.int32, (L, GROUP_W), 0)
    col = lax.broadcasted_iota(jnp.int32, (L, GROUP_W), 1) & (HEAD_A - 1)
    r256 = lax.broadcasted_iota(jnp.int32, (GROUP_W, GROUP_W), 0)
    c256 = lax.broadcasted_iota(jnp.int32, (GROUP_W, GROUP_W), 1)
    same_head = (r256 // HEAD_A) == (c256 // HEAD_A)
    eye_ss = jnp.where(row == col, 1.0, 0.0)
    strict = row > col
    incl = row >= col
    level_masks = []
    s = 1
    while s < L:
        level_masks.append(((row ^ col) < 2 * s) & ((row & s) > (col & s)))
        s *= 2

    lane = lax.broadcasted_iota(jnp.int32, (L, LANES), 1)
    half_masks = [jnp.where(lane < HEAD_A, 1.0, 0.0).astype(BF16), jnp.where(lane >= HEAD_A, 1.0, 0.0).astype(BF16)]
    zero_tile = jnp.zeros((L, LANES), BF16)

    def bd(pieces):
        out = []
        for p in pieces:
            blocks = []
            for h in range(reps):
                lt = h * HEAD_A // LANES
                half = p[:, lt * LANES:(lt + 1) * LANES] * half_masks[h % 2]
                blocks.append(jnp.concatenate([half if t == lt else zero_tile for t in range(GROUP_W // LANES)],
                                              axis=1))
            out.append(jnp.concatenate(blocks, axis=0))
        return out

    def per_head(xs, y, n):
        return _mm_pieces(xs, bd(_split(y, n)))

    def fold(full):
        masked = jnp.where(same_head, full, 0.0)
        return sum(masked[h * L:(h + 1) * L] for h in range(reps))

    cgs = [(ci, gi) for ci in range(TT // L) for gi in range(N_GROUPS)]
    sl = {cg: (slice(cg[0] * L, (cg[0] + 1) * L), slice(cg[1] * GROUP_W, (cg[1] + 1) * GROUP_W)) for cg in cgs}
    kt_p, a_ab, a_ak, a_rb, a_rk, t_inv, r_hat, y0, m_ss, c_ss = ({} for _ in range(10))

    def scores():
        for cg in cgs:
            kt_p[cg] = _split(kt_s[sl[cg]], max(N_SCORE, N_APPLY))
            rt_p = _split(rt_s[sl[cg]], N_SCORE)
            lhs_p = [jnp.concatenate([a, b], axis=0) for a, b in zip(kt_p[cg][:N_SCORE], rt_p)]
            s_b = _mm_pieces(lhs_p, bd(_split(bt_s[sl[cg]], N_SCORE)), _NT)
            s_k = _mm_pieces(lhs_p, bd(_split(kd_s[sl[cg]], N_SCORE)), _NT)
            a_ab[cg] = s_b[:L]
            a_ak[cg] = jnp.where(strict, s_k[:L], 0.0)
            a_rb[cg] = jnp.where(incl, s_b[L:], 0.0)
            a_rk[cg] = jnp.where(incl, s_k[L:], 0.0)
            t_inv[cg] = eye_ss - jnp.where(level_masks[0], a_ab[cg], 0.0)

    def inverse_level(m):
        def run():
            t_p = {cg: _split(t_inv[cg], N_INV) for cg in cgs}
            left = {cg: per_head(t_p[cg], jnp.where(m, a_ab[cg], 0.0), N_INV) for cg in cgs}
            for cg in cgs:
                t_inv[cg] = t_inv[cg] - _mm_pieces(_split(left[cg], N_INV), bd(t_p[cg]))
        return run

    def apply():
        t_p = {cg: _split(t_inv[cg], N_APPLY) for cg in cgs}
        v_bd = {cg: bd(_split(v_s[sl[cg]], N_APPLY)) for cg in cgs}
        w_mat = {cg: -_mm_pieces(t_p[cg], bd(kt_p[cg][:N_APPLY])) for cg in cgs}
        akv = {cg: _mm_pieces(_split(a_ak[cg], N_APPLY), v_bd[cg]) for cg in cgs}
        u0 = {cg: -per_head(t_p[cg], akv[cg], N_APPLY) for cg in cgs}
        arb_p = {cg: _split(a_rb[cg], N_APPLY) for cg in cgs}
        for cg in cgs:
            r_hat[cg] = rt_s[sl[cg]] + per_head(arb_p[cg], w_mat[cg], N_APPLY)
            y0[cg] = per_head(arb_p[cg], u0[cg], N_APPLY) + _mm_pieces(_split(a_rk[cg], N_APPLY), v_bd[cg])
        for cg in cgs:
            be_p = _split(be_s[sl[cg]], N_TRANS)
            ci = cg[0]
            p_end = jnp.exp(cum_s[ci * L + L - 1:ci * L + L, sl[cg][1]])
            m_ss[cg] = fold(_mm_pieces(be_p, _split(w_mat[cg], N_TRANS), _TN)) + eye_ss * p_end
            c_ss[cg] = fold(_mm_pieces(be_p, _split(u0[cg], N_TRANS), _TN)
                            + _mm_pieces(_split(ke_s[sl[cg]], N_TRANS), _split(v_s[sl[cg]], N_TRANS), _TN))

    def state():
        for cg in cgs:
            gi = cg[1]
            st_bd = bd(_split(st_ref[gi], N_STATE))
            both = _mm_pieces([jnp.concatenate([a, b], axis=0)
                               for a, b in zip(_split(r_hat[cg], N_STATE), _split(m_ss[cg], N_STATE))], st_bd)
            y_s[sl[cg]] = both[:L] + y0[cg]
            st_ref[gi] = both[L:] + c_ss[cg]

    return [scores] + [inverse_level(m) for m in level_masks[1:]] + [apply, state]


def _seg_sum(x, seg):
    return jnp.concatenate([_mm_pieces(_split(x[:, gi * GROUP_W:(gi + 1) * GROUP_W], N_SUM), [seg])
                            for gi in range(N_GROUPS)], axis=1)


def _even_kernel(*refs, has_vres):
    (xc_ref, xp_ref, gn_ref, win_ref, mu_ref, w0_ref, wup_ref, a0_ref, aup_ref, gup_ref, kk_ref, ka_ref, rk_ref,
     lnw_ref, lnb_ref, seg_ref, cw_ref, wo_ref) = refs[:18]
    n_in = 18
    if has_vres:
        vf_ref, v0_ref, vdn_ref, vup_ref = refs[18:22]
        n_in = 22
        o_ref = refs[n_in]
        scratch = refs[n_in + 1:]
    else:
        o_ref, vout_ref = refs[n_in:n_in + 2]
        scratch = refs[n_in + 2:]
    (ext_ref, extc_ref, bonus_s, g_s, yb_s, v_s, kt_s, rt_s, bt_s, kd_s, be_s, ke_s, cum_s, y_s, st_ref) = scratch
    L = CHUNK
    i = pl.program_id(0)
    n_tiles = pl.num_programs(0) - 1

    @pl.when(i == 0)
    def _():
        ext_ref[0:HALO, :] = jnp.zeros((HALO, D_IN_A), F32)
        extc_ref[0:HALO, :] = jnp.zeros((HALO, D_B), F32)
        st_ref[...] = jnp.zeros_like(st_ref)
        for ref in (bonus_s, g_s, yb_s, v_s, kt_s, rt_s, bt_s, kd_s, be_s, ke_s, cum_s):
            ref[...] = jnp.zeros_like(ref)

    late = _rwkv_chunks(kt_s, rt_s, bt_s, kd_s, be_s, ke_s, v_s, cum_s, y_s, st_ref)

    def finish():
        y = y_s[...]
        inv_n = 1.0 / HEAD_A
        mean = _seg_sum(y, seg_ref[...]) * inv_n
        d = y - mean
        var = _seg_sum(d * d, seg_ref[...]) * inv_n
        y = d * lax.rsqrt(var + GN_EPS) * lnw_ref[...] + lnb_ref[...]
        ya = (y + bonus_s[...]) * g_s[...]
        mix = _bdot(jnp.concatenate([ya, yb_s[...]], axis=-1), wo_ref[...])
        o_ref[...] = xp_ref[...] + mix

    e = {}

    def project(c0, c1):
        p = jnp.dot(e["h"], win_ref[:, c0:c1], preferred_element_type=F32)
        ext_ref[HALO:HALO + TT, c0:c1] = p
        prev = ext_ref[HALO - 1:HALO - 1 + TT, c0:c1]
        ext_ref[0:HALO, c0:c1] = p[TT - HALO:TT, :]
        return p + (prev - p) * mu_ref[:, c0:c1]

    def early_keys():
        e["h"] = _rmsnorm(xc_ref[...], gn_ref[...]).astype(BF16)
        k = project(D_A, 2 * D_A)
        lx = project(3 * D_A, D_IN_A)
        x_wa = lx[:, :LANES]
        z = w0_ref[...] + _bdot(jnp.tanh(x_wa), wup_ref[...])
        e["wl"] = -DECAY_SCALE * _sigmoid(z)
        a = _sigmoid(a0_ref[...] + _bdot(x_wa, aup_ref[...]))
        e["g"] = _bdot(_sigmoid(lx[:, LANES:]), gup_ref[...])
        kn = k * kk_ref[...]
        e["kn"] = kn * lax.rsqrt(jnp.maximum(_seg_sum(kn * kn, seg_ref[...]), 1e-24))
        e["k"] = k * (1.0 + (a - 1.0) * ka_ref[...])
        e["b"] = e["kn"] * a

    def early_decay():
        ti = lax.broadcasted_iota(jnp.int32, (TT, TT), 0)
        tj = lax.broadcasted_iota(jnp.int32, (TT, TT), 1)
        tri = jnp.where((ti // L) == (tj // L), jnp.where(tj <= ti, 1.0, 0.0), 0.0).astype(BF16)
        cum = _mm_pieces([tri], _split(e["wl"], N_SUM))
        to_end = jnp.concatenate([cum[c * L + L - 1:c * L + L, :] - cum[c * L:(c + 1) * L, :]
                                  for c in range(TT // L)], axis=0)
        inv_p = jnp.exp(-cum)
        e_end = jnp.exp(to_end)
        e["cum"] = cum
        e["be"] = e["b"] * e_end
        e["ke"] = e["k"] * e_end
        kt_s[...] = e["kn"] * jnp.exp(cum - e["wl"])
        bt_s[...] = e["b"] * inv_p
        kd_s[...] = e["k"] * inv_p

    def early_values():
        r = project(0, D_A)
        v = project(2 * D_A, 3 * D_A)
        if has_vres:
            low = _bdot(v, vdn_ref[...])
            v = v + (vf_ref[...] - v) * _sigmoid(v0_ref[...] + _bdot(low, vup_ref[...]))
        e["v"] = v
        e["rt"] = r * jnp.exp(e["cum"])
        e["bonus"] = _seg_sum(r * e["k"] * rk_ref[...], seg_ref[...]) * v

    def early_conv():
        pb = jnp.dot(e["h"], win_ref[:, D_IN_A:], preferred_element_type=F32)
        ch = pb[:, D_B:2 * D_B] * pb[:, 2 * D_B:]
        extc_ref[HALO:HALO + TT, :] = ch
        conv = (cw_ref[0:1, :] * extc_ref[HALO - 2:HALO - 2 + TT, :]
                + cw_ref[1:2, :] * extc_ref[HALO - 1:HALO - 1 + TT, :]
                + cw_ref[2:3, :] * ch)
        extc_ref[0:HALO, :] = ch[TT - HALO:TT, :]
        e["yb"] = pb[:, :D_B] * conv

    early = [early_keys, early_decay, early_values, early_conv]
    for n, stage in enumerate(late):
        stage()
        if n < len(early):
            early[n]()
    finish()

    rt_s[...] = e["rt"]
    v_s[...] = e["v"]
    be_s[...] = e["be"]
    ke_s[...] = e["ke"]
    cum_s[...] = e["cum"]
    bonus_s[...] = e["bonus"]
    g_s[...] = e["g"]
    yb_s[...] = e["yb"]
    if not has_vres:
        @pl.when(i < n_tiles)
        def _():
            vout_ref[...] = e["v"]


def _even_mix(x, g, prm, v_first, vres):
    T = x.shape[0]
    n_tiles = T // TT
    has_vres = vres is not None
    cur = lambda w: pl.BlockSpec((TT, w), lambda i: (jnp.minimum(i, n_tiles - 1), 0))
    prv = lambda w: pl.BlockSpec((TT, w), lambda i: (jnp.maximum(i - 1, 0), 0))
    vec = _resident((1, D_A))
    lora = _resident((LANES, D_A))
    in_specs = [cur(D_MODEL), prv(D_MODEL), _resident((1, D_MODEL)), _resident((D_MODEL, D_IN_A + 3 * D_B)),
                _resident((1, D_IN_A)), vec, lora, vec, lora, lora, vec, vec, vec, vec, vec,
                _resident((GROUP_W, GROUP_W)), _resident(prm["conv_w"].shape), _resident((D_MODEL, D_MODEL))]
    args = [x, x, g, prm["w_in"], prm["mu"], prm["w0"], prm["w_up"], prm["a0"], prm["a_up"], prm["g_up"], prm["k_k"],
            prm["k_a"], prm["r_k"], prm["ln_w"], prm["ln_b"], prm["seg"], prm["conv_w"], prm["w_out"]]
    out_specs = [prv(D_MODEL)]
    out_shape = [jax.ShapeDtypeStruct((T, D_MODEL), F32)]
    if has_vres:
        in_specs += [cur(D_A), vec, _resident((D_A, LANES)), lora]
        args += [v_first, vres["v0"], vres["v_dn"], vres["v_up"]]
    else:
        out_specs.append(cur(D_A))
        out_shape.append(jax.ShapeDtypeStruct((T, D_A), F32))
    tile = pltpu.VMEM((TT, D_A), F32)
    scratch = ([pltpu.VMEM((HALO + TT, D_IN_A), F32), pltpu.VMEM((HALO + TT, D_B), F32)] + [tile] * 12
               + [pltpu.VMEM((N_GROUPS, CHUNK, GROUP_W), F32)])
    outs = pl.pallas_call(
        functools.partial(_even_kernel, has_vres=has_vres),
        grid=(n_tiles + 1,),
        in_specs=in_specs,
        out_specs=out_specs,
        out_shape=out_shape,
        scratch_shapes=scratch,
        compiler_params=_params(),
        name="even_mix",
    )(*args)
    return (outs[0], v_first) if has_vres else (outs[0], outs[1])


def _gelu_tanh(x):
    return x * (0.5 * (1.0 + jnp.tanh(0.7978845608028654 * (x + 0.044715 * (x * x * x)))))


def _linear_scan(a, b, h0):
    n, c = a.shape
    nb = n // SUBLANES
    a = a.reshape(nb, SUBLANES, c)
    b = b.reshape(nb, SUBLANES, c)
    sub = lax.broadcasted_iota(jnp.int32, (nb, SUBLANES, c), 1)
    step = 1
    while step < SUBLANES:
        valid = sub >= step
        a_prev = jnp.where(valid, pltpu.roll(a, step, 1), 1.0)
        b_prev = jnp.where(valid, pltpu.roll(b, step, 1), 0.0)
        b = a * b_prev + b
        a = a * a_prev
        step *= 2
    carry = h0
    blocks = []
    for j in range(nb):
        hj = a[j] * carry + b[j]
        blocks.append(hj)
        carry = hj[SUBLANES - 1:SUBLANES, :]
    return jnp.concatenate(blocks, axis=0)


def _odd_kernel(x_ref, g_ref, win_ref, cw_ref, cb_ref, wa_ref, ba_ref, wx_ref, bx_ref, lam_ref, dw_ref, ds_ref,
                wo_ref, o_ref, extu_ref, extd_ref, h_ref):
    i = pl.program_id(0)

    @pl.when(i == 0)
    def _():
        extu_ref[0:HALO, :] = jnp.zeros((HALO, D_C), F32)
        extd_ref[0:POOL_HALO, :] = jnp.zeros((POOL_HALO, D_D), F32)
        h_ref[...] = jnp.zeros_like(h_ref)

    x = x_ref[...]
    h = _rmsnorm(x, g_ref[...]).astype(BF16)
    pc = jnp.dot(h, win_ref[:, :2 * D_C], preferred_element_type=F32)
    pd = jnp.dot(h, win_ref[:, 2 * D_C:], preferred_element_type=F32)
    t_glob = i * TT + lax.broadcasted_iota(jnp.int32, (TT, LANES), 0)

    gate = pc[:, :D_C]
    u_in = pc[:, D_C:]
    extu_ref[HALO:HALO + TT, :] = u_in
    u = cb_ref[...] + cw_ref[3:4, :] * u_in
    for j in range(3):
        u = u + cw_ref[j:j + 1, :] * extu_ref[HALO - 3 + j:HALO - 3 + j + TT, :]
    extu_ref[0:HALO, :] = u_in[TT - HALO:TT, :]
    rec = _sigmoid(_bdot(u, wa_ref[...]) + ba_ref[...])
    inp = _sigmoid(_bdot(u, wx_ref[...]) + bx_ref[...])
    log_a = (-LRU_C) * rec * _softplus(-lam_ref[...])
    a = jnp.exp(log_a)
    mult = jnp.sqrt(-jnp.tanh(log_a) * (a * a + 1.0))
    row = lax.broadcasted_iota(jnp.int32, (TT, D_C), 0)
    mult = jnp.where(row + i * TT == 0, 1.0, mult)
    hs = _linear_scan(a, mult * inp * u, h_ref[...])
    h_ref[...] = hs[TT - 1:TT, :]
    yc = _gelu_tanh(gate) * hs

    extd_ref[POOL_HALO:POOL_HALO + TT, :] = pd
    parts = []
    for gi, win in enumerate(POOL_WINDOWS):
        e = extd_ref[:, gi * G_D:(gi + 1) * G_D]
        span = 1
        while span < win:
            e = e + pltpu.roll(e, span, 0)
            span *= 2
        n_avail = jnp.minimum(t_glob + 1, win).astype(F32)
        parts.append(e[POOL_HALO:, :] / n_avail - pd[:, gi * G_D:(gi + 1) * G_D])
    extd_ref[0:POOL_HALO, :] = pd[TT - POOL_HALO:TT, :]
    yd = _bdot(jnp.concatenate(parts, axis=-1), dw_ref[...]) * ds_ref[...]

    mix = _bdot(jnp.concatenate([yc, yd], axis=-1), wo_ref[...])
    o_ref[...] = x + mix


def _odd_mix(x, g, prm):
    T = x.shape[0]
    vec = _resident((1, D_C))
    sq = _resident((D_C, D_C))
    return pl.pallas_call(
        _odd_kernel,
        grid=(T // TT,),
        in_specs=[_rows(D_MODEL, TT), _resident((1, D_MODEL)), _resident((D_MODEL, 2 * D_C + D_D)),
                  _resident(prm["conv_w"].shape), vec, sq, vec, sq, vec, vec, sq, vec,
                  _resident((D_MODEL, D_MODEL))],
        out_specs=_rows(D_MODEL, TT),
        out_shape=jax.ShapeDtypeStruct((T, D_MODEL), F32),
        scratch_shapes=[pltpu.VMEM((HALO + TT, D_C), F32), pltpu.VMEM((POOL_HALO + TT, D_D), F32),
                        pltpu.VMEM((1, D_C), F32)],
        compiler_params=_params(),
        name="odd_mix",
    )(x, g, prm["w_in"], prm["conv_w"], prm["conv_b"], prm["wa"], prm["ba"], prm["wx"], prm["bx"], prm["lam"],
      prm["dw"], prm["ds"], prm["w_out"])


def _block_diag(w):
    H, n, m = w.shape
    return jnp.einsum("hij,hg->higj", w, jnp.eye(H, dtype=w.dtype)).reshape(H * n, H * m)


def _pad_rows(w, before, total):
    return jnp.pad(w, ((before, total - before - w.shape[0]), (0, 0)))


def kernel(x, norm_g, ffn_wg, ffn_wu, ffn_wd, even_w_in, even_w_out, a_mu, a_w0, a_w_up, a_a0, a_a_up, a_g_up,
           a_k_k, a_k_a, a_r_k, a_ln_w, a_ln_b, a_v0, a_v_dn, a_v_up, b_conv_w, odd_w_in, odd_w_out, c_conv_w,
           c_conv_b, c_wa, c_ba, c_wx, c_bx, c_lam, d_w, d_scale, final_g):
    B, T, D = x.shape
    assert (B, D) == (1, D_MODEL) and T % TM_FFN == 0 and T % TT == 0
    xt = x.reshape(T, D)
    ffn_f32 = (ffn_wg, ffn_wu, ffn_wd)
    w_ffn = tuple(w[0, 0].astype(BF16) for w in ffn_f32)
    n_ffn = 2 * DEPTH
    following = lambda j: (*ffn_f32, (j + 1) // 2, (j + 1) % 2) if j + 1 < n_ffn else None
    row = lambda v: v.reshape(1, -1)
    lora_w = a_w_up.shape[1]
    seg = _block_diag(jnp.ones((GROUP_W // HEAD_A, HEAD_A, HEAD_A), BF16))
    v_first = None
    for layer in range(DEPTH):
        xt, w_ffn = _ffn(xt, row(norm_g[layer, 0]), w_ffn, following(2 * layer))
        g_mix = row(norm_g[layer, 1])
        if layer % 2 == 0:
            e = layer // 2
            prm = dict(
                w_in=even_w_in[e].astype(BF16), w_out=even_w_out[e].astype(BF16), conv_w=b_conv_w[e],
                mu=row(a_mu[e]), w0=row(a_w0[e]), a0=row(a_a0[e]),
                w_up=_pad_rows(a_w_up[e], 0, LANES).astype(BF16),
                a_up=_pad_rows(a_a_up[e], lora_w, LANES).astype(BF16),
                g_up=a_g_up[e].astype(BF16),
                k_k=row(a_k_k[e]), k_a=row(a_k_a[e]), r_k=row(a_r_k[e]), ln_w=row(a_ln_w[e]), ln_b=row(a_ln_b[e]),
                seg=seg)
            vres = None
            if e > 0:
                vres = dict(v0=row(a_v0[e - 1]),
                            v_dn=jnp.pad(a_v_dn[e - 1], ((0, 0), (0, LANES - LORA_V))).astype(BF16),
                            v_up=_pad_rows(a_v_up[e - 1], 0, LANES).astype(BF16))
            xt, v_first = _even_mix(xt, g_mix, prm, v_first, vres)
        else:
            o = layer // 2
            prm = dict(
                w_in=odd_w_in[o].astype(BF16), w_out=odd_w_out[o].astype(BF16),
                conv_w=c_conv_w[o], conv_b=row(c_conv_b[o]),
                wa=_block_diag(c_wa[o]).astype(BF16), ba=row(c_ba[o]),
                wx=_block_diag(c_wx[o]).astype(BF16), bx=row(c_bx[o]), lam=row(c_lam[o]),
                dw=_block_diag(d_w[o]).astype(BF16), ds=row(d_scale[o]))
            xt = _odd_mix(xt, g_mix, prm)
        last = layer == DEPTH - 1
        xt, w_ffn = _ffn(xt, row(norm_g[layer, 2]), w_ffn, following(2 * layer + 1),
                         final_g=row(final_g) if last else None)
    return xt.reshape(B, T, D)
```

```python
import functools

import jax
import jax.numpy as jnp
from jax import lax
from jax.experimental import pallas as pl
from jax.experimental.pallas import tpu as pltpu

F32 = jnp.float32
BF16 = jnp.bfloat16

D_MODEL = 1024
DEPTH = 4
D_A = 512
HEAD_A = 64
LORA_V = 32
D_IN_A = 1792
D_B = 512
D_C = 512
H_C = 8
D_D = 512
POOL_WINDOWS = (2, 4, 8, 16)
G_D = 128
D_FF = 2816
GN_EPS = 64e-5
RMS_EPS = 1e-6
LRU_C = 8.0
DECAY_SCALE = 0.6065306597126334

LANES = 128
SUBLANES = 8
MXU_DIM = 256
VMEM_LIMIT_BYTES = 56 * 1024 * 1024

TM_FFN = 1024
FF_CHUNK = MXU_DIM
TT = 256
CHUNK = 64
GROUP_W = MXU_DIM
N_GROUPS = D_A // GROUP_W
HALO = SUBLANES
POOL_HALO = 2 * SUBLANES

N_SCORE = 1
N_INV = 1
N_APPLY = 1
N_TRANS = 1
N_STATE = 2
N_SUM = 2

_NN = (((1,), (0,)), ((), ()))
_NT = (((1,), (1,)), ((), ()))
_TN = (((0,), (0,)), ((), ()))


def _params(n_axes=1):
    return pltpu.CompilerParams(dimension_semantics=("arbitrary",) * n_axes,
                                vmem_limit_bytes=VMEM_LIMIT_BYTES)


def _resident(shape):
    zeros = (0,) * len(shape)
    return pl.BlockSpec(shape, lambda i: zeros, pipeline_mode=pl.Buffered(1))


def _rows(width, tile):
    return pl.BlockSpec((tile, width), lambda i: (i, 0))


def _rmsnorm(x, g):
    ms = jnp.mean(x * x, axis=-1, keepdims=True)
    return x * lax.rsqrt(ms + RMS_EPS) * g


def _sigmoid(x):
    return 1.0 / (1.0 + jnp.exp(-x))


def _softplus(z):
    return jnp.maximum(z, 0.0) + jnp.log1p(jnp.exp(-jnp.abs(z)))


def _bdot(x, w):
    return jnp.dot(x.astype(BF16), w, preferred_element_type=F32)


def _split(x, n):
    pieces = []
    for _ in range(n - 1):
        p = x.astype(BF16)
        pieces.append(p)
        x = x - p.astype(F32)
    pieces.append(x.astype(BF16))
    return pieces


def _mm_pieces(xs, ys, dims=_NN):
    n = max(len(xs), len(ys))
    axis = 1 if dims == _TN else 0
    rows = xs[0].shape[axis]
    out = None
    for j, y in enumerate(ys):
        lhs = xs[:n - j]
        if not lhs:
            continue
        stacked = lhs[0] if len(lhs) == 1 else jnp.concatenate(lhs, axis=axis)
        prod = lax.dot_general(stacked, y, dims, preferred_element_type=F32)
        for i in range(len(lhs)):
            part = prod[i * rows:(i + 1) * rows]
            out = part if out is None else out + part
    return out


def _ffn_kernel(*refs, final_norm, cast_next):
    x_ref, g_ref, wg_ref, wu_ref, wd_ref, fg_ref = refs[:6]
    if cast_next:
        o_ref = refs[9]
        for src, dst in zip(refs[6:9], refs[10:13]):
            dst[...] = src[...].astype(BF16)
    else:
        o_ref = refs[6]
    act_ref = refs[-1]
    x = x_ref[...]
    h = _rmsnorm(x, g_ref[...]).astype(BF16)
    for j in range(D_FF // FF_CHUNK):
        cols = slice(j * FF_CHUNK, (j + 1) * FF_CHUNK)
        gate = jnp.dot(h, wg_ref[:, cols], preferred_element_type=F32)
        up = jnp.dot(h, wu_ref[:, cols], preferred_element_type=F32)
        act_ref[:, cols] = (gate * _sigmoid(gate) * up).astype(BF16)
    y = x + 0.5 * jnp.dot(act_ref[...], wd_ref[...], preferred_element_type=F32)
    if final_norm:
        y = _rmsnorm(y, fg_ref[...])
    o_ref[...] = y


def _ffn(x, g, weights, next_f32=None, final_g=None):
    T = x.shape[0]
    n_steps = T // TM_FFN
    fg = g if final_g is None else final_g
    in_specs = [_rows(D_MODEL, TM_FFN), _resident((1, D_MODEL)), _resident((D_MODEL, D_FF)),
                _resident((D_MODEL, D_FF)), _resident((D_FF, D_MODEL)), _resident((1, D_MODEL))]
    args = [x, g, *weights, fg]
    out_specs = [_rows(D_MODEL, TM_FFN)]
    out_shape = [jax.ShapeDtypeStruct((T, D_MODEL), F32)]
    if next_f32 is not None:
        *w_next, layer, k = next_f32
        for w in w_next:
            rows, cols = w.shape[2] // n_steps, w.shape[3]
            in_specs.append(pl.BlockSpec((None, None, rows, cols), lambda i: (layer, k, i, 0)))
            out_specs.append(pl.BlockSpec((rows, cols), lambda i: (i, 0)))
            out_shape.append(jax.ShapeDtypeStruct(w.shape[2:], BF16))
        args += w_next
    outs = pl.pallas_call(
        functools.partial(_ffn_kernel, final_norm=final_g is not None, cast_next=next_f32 is not None),
        grid=(n_steps,),
        in_specs=in_specs,
        out_specs=out_specs,
        out_shape=out_shape,
        scratch_shapes=[pltpu.VMEM((TM_FFN, D_FF), BF16)],
        compiler_params=_params(),
        name="ffn",
    )(*args)
    return outs[0], tuple(outs[1:])


def _rwkv_chunks(kt_s, rt_s, bt_s, kd_s, be_s, ke_s, v_s, cum_s, y_s, st_ref):
    L = CHUNK
    reps = GROUP_W // L
    row = lax.broadcasted_iota(jnp.int32, (L, GROUP_W), 0)
    col = lax.broadcasted_iota(jnp.int32, (L, GROUP_W), 1) & (HEAD_A - 1)
    r256 = lax.broadcasted_iota(jnp.int32, (GROUP_W, GROUP_W), 0)
    c256 = lax.broadcasted_iota(jnp.int32, (GROUP_W, GROUP_W), 1)
    same_head = (r256 // HEAD_A) == (c256 // HEAD_A)
    eye_ss = jnp.where(row == col, 1.0, 0.0)
    strict = row > col
    incl = row >= col
    level_masks = []
    s = 1
    while s < L:
        level_masks.append(((row ^ col) < 2 * s) & ((row & s) > (col & s)))
        s *= 2

    lane = lax.broadcasted_iota(jnp.int32, (L, LANES), 1)
    half_masks = [jnp.where(lane < HEAD_A, 1.0, 0.0).astype(BF16), jnp.where(lane >= HEAD_A, 1.0, 0.0).astype(BF16)]
    zero_tile = jnp.zeros((L, LANES), BF16)

    def bd(pieces):
        out = []
        for p in pieces:
            blocks = []
            for h in range(reps):
                lt = h * HEAD_A // LANES
                half = p[:, lt * LANES:(lt + 1) * LANES] * half_masks[h % 2]
                blocks.append(jnp.concatenate([half if t == lt else zero_tile for t in range(GROUP_W // LANES)],
                                              axis=1))
            out.append(jnp.concatenate(blocks, axis=0))
        return out

    def per_head(xs, y, n):
        return _mm_pieces(xs, bd(_split(y, n)))

    def fold(full):
        masked = jnp.where(same_head, full, 0.0)
        return sum(masked[h * L:(h + 1) * L] for h in range(reps))

    cgs = [(ci, gi) for ci in range(TT // L) for gi in range(N_GROUPS)]
    sl = {cg: (slice(cg[0] * L, (cg[0] + 1) * L), slice(cg[1] * GROUP_W, (cg[1] + 1) * GROUP_W)) for cg in cgs}
    kt_p, a_ab, a_ak, a_rb, a_rk, t_inv, r_hat, y0, m_ss, c_ss = ({} for _ in range(10))

    def scores():
        for cg in cgs:
            kt_p[cg] = _split(kt_s[sl[cg]], max(N_SCORE, N_APPLY))
            rt_p = _split(rt_s[sl[cg]], N_SCORE)
            lhs_p = [jnp.concatenate([a, b], axis=0) for a, b in zip(kt_p[cg][:N_SCORE], rt_p)]
            s_b = _mm_pieces(lhs_p, bd(_split(bt_s[sl[cg]], N_SCORE)), _NT)
            s_k = _mm_pieces(lhs_p, bd(_split(kd_s[sl[cg]], N_SCORE)), _NT)
            a_ab[cg] = s_b[:L]
            a_ak[cg] = jnp.where(strict, s_k[:L], 0.0)
            a_rb[cg] = jnp.where(incl, s_b[L:], 0.0)
            a_rk[cg] = jnp.where(incl, s_k[L:], 0.0)
            t_inv[cg] = eye_ss - jnp.where(level_masks[0], a_ab[cg], 0.0)

    def inverse_level(m):
        def run():
            t_p = {cg: _split(t_inv[cg], N_INV) for cg in cgs}
            left = {cg: per_head(t_p[cg], jnp.where(m, a_ab[cg], 0.0), N_INV) for cg in cgs}
            for cg in cgs:
                t_inv[cg] = t_inv[cg] - _mm_pieces(_split(left[cg], N_INV), bd(t_p[cg]))
        return run

    def apply():
        t_p = {cg: _split(t_inv[cg], N_APPLY) for cg in cgs}
        v_bd = {cg: bd(_split(v_s[sl[cg]], N_APPLY)) for cg in cgs}
        w_mat = {cg: -_mm_pieces(t_p[cg], bd(kt_p[cg][:N_APPLY])) for cg in cgs}
        akv = {cg: _mm_pieces(_split(a_ak[cg], N_APPLY), v_bd[cg]) for cg in cgs}
        u0 = {cg: -per_head(t_p[cg], akv[cg], N_APPLY) for cg in cgs}
        arb_p = {cg: _split(a_rb[cg], N_APPLY) for cg in cgs}
        for cg in cgs:
            r_hat[cg] = rt_s[sl[cg]] + per_head(arb_p[cg], w_mat[cg], N_APPLY)
            y0[cg] = per_head(arb_p[cg], u0[cg], N_APPLY) + _mm_pieces(_split(a_rk[cg], N_APPLY), v_bd[cg])
        for cg in cgs:
            be_p = _split(be_s[sl[cg]], N_TRANS)
            ci = cg[0]
            p_end = jnp.exp(cum_s[ci * L + L - 1:ci * L + L, sl[cg][1]])
            m_ss[cg] = fold(_mm_pieces(be_p, _split(w_mat[cg], N_TRANS), _TN)) + eye_ss * p_end
            c_ss[cg] = fold(_mm_pieces(be_p, _split(u0[cg], N_TRANS), _TN)
                            + _mm_pieces(_split(ke_s[sl[cg]], N_TRANS), _split(v_s[sl[cg]], N_TRANS), _TN))

    def state():
        for cg in cgs:
            gi = cg[1]
            st_bd = bd(_split(st_ref[gi], N_STATE))
            both = _mm_pieces([jnp.concatenate([a, b], axis=0)
                               for a, b in zip(_split(r_hat[cg], N_STATE), _split(m_ss[cg], N_STATE))], st_bd)
            y_s[sl[cg]] = both[:L] + y0[cg]
            st_ref[gi] = both[L:] + c_ss[cg]

    return [scores] + [inverse_level(m) for m in level_masks[1:]] + [apply, state]


def _seg_sum(x, seg):
    return jnp.concatenate([_mm_pieces(_split(x[:, gi * GROUP_W:(gi + 1) * GROUP_W], N_SUM), [seg])
                            for gi in range(N_GROUPS)], axis=1)


def _even_kernel(*refs, has_vres):
    (xc_ref, xp_ref, gn_ref, win_ref, mu_ref, w0_ref, wup_ref, a0_ref, aup_ref, gup_ref, kk_ref, ka_ref, rk_ref,
     lnw_ref, lnb_ref, seg_ref, cw_ref, wo_ref) = refs[:18]
    n_in = 18
    if has_vres:
        vf_ref, v0_ref, vdn_ref, vup_ref = refs[18:22]
        n_in = 22
        o_ref = refs[n_in]
        scratch = refs[n_in + 1:]
    else:
        o_ref, vout_ref = refs[n_in:n_in + 2]
        scratch = refs[n_in + 2:]
    (ext_ref, extc_ref, bonus_s, g_s, yb_s, v_s, kt_s, rt_s, bt_s, kd_s, be_s, ke_s, cum_s, y_s, st_ref) = scratch
    L = CHUNK
    i = pl.program_id(0)
    n_tiles = pl.num_programs(0) - 1

    @pl.when(i == 0)
    def _():
        ext_ref[0:HALO, :] = jnp.zeros((HALO, D_IN_A), F32)
        extc_ref[0:HALO, :] = jnp.zeros((HALO, D_B), F32)
        st_ref[...] = jnp.zeros_like(st_ref)
        for ref in (bonus_s, g_s, yb_s, v_s, kt_s, rt_s, bt_s, kd_s, be_s, ke_s, cum_s):
            ref[...] = jnp.zeros_like(ref)

    late = _rwkv_chunks(kt_s, rt_s, bt_s, kd_s, be_s, ke_s, v_s, cum_s, y_s, st_ref)

    def finish():
        y = y_s[...]
        inv_n = 1.0 / HEAD_A
        mean = _seg_sum(y, seg_ref[...]) * inv_n
        d = y - mean
        var = _seg_sum(d * d, seg_ref[...]) * inv_n
        y = d * lax.rsqrt(var + GN_EPS) * lnw_ref[...] + lnb_ref[...]
        ya = (y + bonus_s[...]) * g_s[...]
        mix = _bdot(jnp.concatenate([ya, yb_s[...]], axis=-1), wo_ref[...])
        o_ref[...] = xp_ref[...] + mix

    e = {}

    def project(c0, c1):
        p = jnp.dot(e["h"], win_ref[:, c0:c1], preferred_element_type=F32)
        ext_ref[HALO:HALO + TT, c0:c1] = p
        prev = ext_ref[HALO - 1:HALO - 1 + TT, c0:c1]
        ext_ref[0:HALO, c0:c1] = p[TT - HALO:TT, :]
        return p + (prev - p) * mu_ref[:, c0:c1]

    def early_keys():
        e["h"] = _rmsnorm(xc_ref[...], gn_ref[...]).astype(BF16)
        k = project(D_A, 2 * D_A)
        lx = project(3 * D_A, D_IN_A)
        x_wa = lx[:, :LANES]
        z = w0_ref[...] + _bdot(jnp.tanh(x_wa), wup_ref[...])
        e["wl"] = -DECAY_SCALE * _sigmoid(z)
        a = _sigmoid(a0_ref[...] + _bdot(x_wa, aup_ref[...]))
        e["g"] = _bdot(_sigmoid(lx[:, LANES:]), gup_ref[...])
        kn = k * kk_ref[...]
        e["kn"] = kn * lax.rsqrt(jnp.maximum(_seg_sum(kn * kn, seg_ref[...]), 1e-24))
        e["k"] = k * (1.0 + (a - 1.0) * ka_ref[...])
        e["b"] = e["kn"] * a

    def early_decay():
        ti = lax.broadcasted_iota(jnp.int32, (TT, TT), 0)
        tj = lax.broadcasted_iota(jnp.int32, (TT, TT), 1)
        tri = jnp.where((ti // L) == (tj // L), jnp.where(tj <= ti, 1.0, 0.0), 0.0).astype(BF16)
        cum = _mm_pieces([tri], _split(e["wl"], N_SUM))
        to_end = jnp.concatenate([cum[c * L + L - 1:c * L + L, :] - cum[c * L:(c + 1) * L, :]
                                  for c in range(TT // L)], axis=0)
        inv_p = jnp.exp(-cum)
        e_end = jnp.exp(to_end)
        e["cum"] = cum
        e["be"] = e["b"] * e_end
        e["ke"] = e["k"] * e_end
        kt_s[...] = e["kn"] * jnp.exp(cum - e["wl"])
        bt_s[...] = e["b"] * inv_p
        kd_s[...] = e["k"] * inv_p

    def early_values():
        r = project(0, D_A)
        v = project(2 * D_A, 3 * D_A)
        if has_vres:
            low = _bdot(v, vdn_ref[...])
            v = v + (vf_ref[...] - v) * _sigmoid(v0_ref[...] + _bdot(low, vup_ref[...]))
        e["v"] = v
        e["rt"] = r * jnp.exp(e["cum"])
        e["bonus"] = _seg_sum(r * e["k"] * rk_ref[...], seg_ref[...]) * v

    def early_conv():
        pb = jnp.dot(e["h"], win_ref[:, D_IN_A:], preferred_element_type=F32)
        ch = pb[:, D_B:2 * D_B] * pb[:, 2 * D_B:]
        extc_ref[HALO:HALO + TT, :] = ch
        conv = (cw_ref[0:1, :] * extc_ref[HALO - 2:HALO - 2 + TT, :]
                + cw_ref[1:2, :] * extc_ref[HALO - 1:HALO - 1 + TT, :]
                + cw_ref[2:3, :] * ch)
        extc_ref[0:HALO, :] = ch[TT - HALO:TT, :]
        e["yb"] = pb[:, :D_B] * conv

    for stage in late:
        stage()
    finish()
    for stage in (early_keys, early_decay, early_values, early_conv):
        stage()
    rt_s[...] = e["rt"]
    v_s[...] = e["v"]
    be_s[...] = e["be"]
    ke_s[...] = e["ke"]
    cum_s[...] = e["cum"]
    bonus_s[...] = e["bonus"]
    g_s[...] = e["g"]
    yb_s[...] = e["yb"]
    if not has_vres:
        @pl.when(i < n_tiles)
        def _():
            vout_ref[...] = e["v"]


def _even_mix(x, g, prm, v_first, vres):
    T = x.shape[0]
    n_tiles = T // TT
    has_vres = vres is not None
    cur = lambda w: pl.BlockSpec((TT, w), lambda i: (jnp.minimum(i, n_tiles - 1), 0))
    prv = lambda w: pl.BlockSpec((TT, w), lambda i: (jnp.maximum(i - 1, 0), 0))
    vec = _resident((1, D_A))
    lora = _resident((LANES, D_A))
    in_specs = [cur(D_MODEL), prv(D_MODEL), _resident((1, D_MODEL)), _resident((D_MODEL, D_IN_A + 3 * D_B)),
                _resident((1, D_IN_A)), vec, lora, vec, lora, lora, vec, vec, vec, vec, vec,
                _resident((GROUP_W, GROUP_W)), _resident(prm["conv_w"].shape), _resident((D_MODEL, D_MODEL))]
    args = [x, x, g, prm["w_in"], prm["mu"], prm["w0"], prm["w_up"], prm["a0"], prm["a_up"], prm["g_up"], prm["k_k"],
            prm["k_a"], prm["r_k"], prm["ln_w"], prm["ln_b"], prm["seg"], prm["conv_w"], prm["w_out"]]
    out_specs = [prv(D_MODEL)]
    out_shape = [jax.ShapeDtypeStruct((T, D_MODEL), F32)]
    if has_vres:
        in_specs += [cur(D_A), vec, _resident((D_A, LANES)), lora]
        args += [v_first, vres["v0"], vres["v_dn"], vres["v_up"]]
    else:
        out_specs.append(cur(D_A))
        out_shape.append(jax.ShapeDtypeStruct((T, D_A), F32))
    tile = pltpu.VMEM((TT, D_A), F32)
    scratch = ([pltpu.VMEM((HALO + TT, D_IN_A), F32), pltpu.VMEM((HALO + TT, D_B), F32)] + [tile] * 12
               + [pltpu.VMEM((N_GROUPS, CHUNK, GROUP_W), F32)])
    outs = pl.pallas_call(
        functools.partial(_even_kernel, has_vres=has_vres),
        grid=(n_tiles + 1,),
        in_specs=in_specs,
        out_specs=out_specs,
        out_shape=out_shape,
        scratch_shapes=scratch,
        compiler_params=_params(),
        name="even_mix",
    )(*args)
    return (outs[0], v_first) if has_vres else (outs[0], outs[1])


def _gelu_tanh(x):
    return x * (0.5 * (1.0 + jnp.tanh(0.7978845608028654 * (x + 0.044715 * (x * x * x)))))


def _linear_scan(a, b, h0):
    n, c = a.shape
    nb = n // SUBLANES
    a = a.reshape(nb, SUBLANES, c)
    b = b.reshape(nb, SUBLANES, c)
    sub = lax.broadcasted_iota(jnp.int32, (nb, SUBLANES, c), 1)
    step = 1
    while step < SUBLANES:
        valid = sub >= step
        a_prev = jnp.where(valid, pltpu.roll(a, step, 1), 1.0)
        b_prev = jnp.where(valid, pltpu.roll(b, step, 1), 0.0)
        b = a * b_prev + b
        a = a * a_prev
        step *= 2
    carry = h0
    blocks = []
    for j in range(nb):
        hj = a[j] * carry + b[j]
        blocks.append(hj)
        carry = hj[SUBLANES - 1:SUBLANES, :]
    return jnp.concatenate(blocks, axis=0)


def _odd_kernel(xc_ref, xp_ref, g_ref, win_ref, cw_ref, cb_ref, wa_ref, ba_ref, wx_ref, bx_ref, lam_ref, dw_ref,
                ds_ref, wo_ref, o_ref, pc_s, pd_s, gact_s, extu_ref, extd_ref, h_ref):
    i = pl.program_id(0)

    @pl.when(i == 0)
    def _():
        extu_ref[0:HALO, :] = jnp.zeros((HALO, D_C), F32)
        extd_ref[0:POOL_HALO, :] = jnp.zeros((POOL_HALO, D_D), F32)
        h_ref[...] = jnp.zeros_like(h_ref)
        pc_s[...] = jnp.zeros_like(pc_s)
        pd_s[...] = jnp.zeros_like(pd_s)

    gact_s[...] = _gelu_tanh(pc_s[:, :D_C])
    extu_ref[HALO:HALO + TT, :] = pc_s[:, D_C:]
    extd_ref[POOL_HALO:POOL_HALO + TT, :] = pd_s[...]

    h = _rmsnorm(xc_ref[...], g_ref[...]).astype(BF16)

    def project(blocks):
        for n in blocks:
            cols = slice(n * MXU_DIM, (n + 1) * MXU_DIM)
            p = jnp.dot(h, win_ref[:, cols], preferred_element_type=F32)
            if n < 2 * D_C // MXU_DIM:
                pc_s[:, cols] = p
            else:
                pd_s[:, n * MXU_DIM - 2 * D_C:(n + 1) * MXU_DIM - 2 * D_C] = p

    t_glob = (i - 1) * TT + lax.broadcasted_iota(jnp.int32, (TT, LANES), 0)

    u_in = extu_ref[HALO:HALO + TT, :]
    u = cb_ref[...] + cw_ref[3:4, :] * u_in
    for j in range(3):
        u = u + cw_ref[j:j + 1, :] * extu_ref[HALO - 3 + j:HALO - 3 + j + TT, :]
    extu_ref[0:HALO, :] = u_in[TT - HALO:TT, :]
    project((0, 1))
    rec = _sigmoid(_bdot(u, wa_ref[...]) + ba_ref[...])
    inp = _sigmoid(_bdot(u, wx_ref[...]) + bx_ref[...])
    log_a = (-LRU_C) * rec * _softplus(-lam_ref[...])
    a = jnp.exp(log_a)
    mult = jnp.sqrt(-jnp.tanh(log_a) * (a * a + 1.0))
    row = lax.broadcasted_iota(jnp.int32, (TT, D_C), 0)
    mult = jnp.where(row + (i - 1) * TT == 0, 1.0, mult)
    project((2, 3))
    hs = _linear_scan(a, mult * inp * u, h_ref[...])
    h_ref[...] = jnp.where(i > 0, hs[TT - 1:TT, :], 0.0)
    yc = gact_s[...] * hs
    project((4, 5))

    parts = []
    for gi, win in enumerate(POOL_WINDOWS):
        e = extd_ref[:, gi * G_D:(gi + 1) * G_D]
        span = 1
        while span < win:
            e = e + pltpu.roll(e, span, 0)
            span *= 2
        n_avail = jnp.clip(t_glob + 1, 1, win).astype(F32)
        parts.append(e[POOL_HALO:, :] / n_avail - extd_ref[POOL_HALO:POOL_HALO + TT, gi * G_D:(gi + 1) * G_D])
    extd_ref[0:POOL_HALO, :] = extd_ref[TT:TT + POOL_HALO, :]
    yd = _bdot(jnp.concatenate(parts, axis=-1), dw_ref[...]) * ds_ref[...]

    mix = _bdot(jnp.concatenate([yc, yd], axis=-1), wo_ref[...])
    o_ref[...] = xp_ref[...] + mix


def _odd_mix(x, g, prm):
    T = x.shape[0]
    n_tiles = T // TT
    cur = pl.BlockSpec((TT, D_MODEL), lambda i: (jnp.minimum(i, n_tiles - 1), 0))
    prv = pl.BlockSpec((TT, D_MODEL), lambda i: (jnp.maximum(i - 1, 0), 0))
    vec = _resident((1, D_C))
    sq = _resident((D_C, D_C))
    return pl.pallas_call(
        _odd_kernel,
        grid=(n_tiles + 1,),
        in_specs=[cur, prv, _resident((1, D_MODEL)), _resident((D_MODEL, 2 * D_C + D_D)),
                  _resident(prm["conv_w"].shape), vec, sq, vec, sq, vec, vec, sq, vec,
                  _resident((D_MODEL, D_MODEL))],
        out_specs=prv,
        out_shape=jax.ShapeDtypeStruct((T, D_MODEL), F32),
        scratch_shapes=[pltpu.VMEM((TT, 2 * D_C), F32), pltpu.VMEM((TT, D_D), F32), pltpu.VMEM((TT, D_C), F32),
                        pltpu.VMEM((HALO + TT, D_C), F32), pltpu.VMEM((POOL_HALO + TT, D_D), F32),
                        pltpu.VMEM((1, D_C), F32)],
        compiler_params=_params(),
        name="odd_mix",
    )(x, x, g, prm["w_in"], prm["conv_w"], prm["conv_b"], prm["wa"], prm["ba"], prm["wx"], prm["bx"], prm["lam"],
      prm["dw"], prm["ds"], prm["w_out"])


def _block_diag(w):
    H, n, m = w.shape
    return jnp.einsum("hij,hg->higj", w, jnp.eye(H, dtype=w.dtype)).reshape(H * n, H * m)


def _pad_rows(w, before, total):
    return jnp.pad(w, ((before, total - before - w.shape[0]), (0, 0)))


def kernel(x, norm_g, ffn_wg, ffn_wu, ffn_wd, even_w_in, even_w_out, a_mu, a_w0, a_w_up, a_a0, a_a_up, a_g_up,
           a_k_k, a_k_a, a_r_k, a_ln_w, a_ln_b, a_v0, a_v_dn, a_v_up, b_conv_w, odd_w_in, odd_w_out, c_conv_w,
           c_conv_b, c_wa, c_ba, c_wx, c_bx, c_lam, d_w, d_scale, final_g):
    B, T, D = x.shape
    assert (B, D) == (1, D_MODEL) and T % TM_FFN == 0 and T % TT == 0
    xt = x.reshape(T, D)
    ffn_f32 = (ffn_wg, ffn_wu, ffn_wd)
    w_ffn = tuple(w[0, 0].astype(BF16) for w in ffn_f32)
    n_ffn = 2 * DEPTH
    following = lambda j: (*ffn_f32, (j + 1) // 2, (j + 1) % 2) if j + 1 < n_ffn else None
    row = lambda v: v.reshape(1, -1)
    lora_w = a_w_up.shape[1]
    seg = _block_diag(jnp.ones((GROUP_W // HEAD_A, HEAD_A, HEAD_A), BF16))
    v_first = None
    for layer in range(DEPTH):
        xt, w_ffn = _ffn(xt, row(norm_g[layer, 0]), w_ffn, following(2 * layer))
        g_mix = row(norm_g[layer, 1])
        if layer % 2 == 0:
            e = layer // 2
            prm = dict(
                w_in=even_w_in[e].astype(BF16), w_out=even_w_out[e].astype(BF16), conv_w=b_conv_w[e],
                mu=row(a_mu[e]), w0=row(a_w0[e]), a0=row(a_a0[e]),
                w_up=_pad_rows(a_w_up[e], 0, LANES).astype(BF16),
                a_up=_pad_rows(a_a_up[e], lora_w, LANES).astype(BF16),
                g_up=a_g_up[e].astype(BF16),
                k_k=row(a_k_k[e]), k_a=row(a_k_a[e]), r_k=row(a_r_k[e]), ln_w=row(a_ln_w[e]), ln_b=row(a_ln_b[e]),
                seg=seg)
            vres = None
            if e > 0:
                vres = dict(v0=row(a_v0[e - 1]),
                            v_dn=jnp.pad(a_v_dn[e - 1], ((0, 0), (0, LANES - LORA_V))).astype(BF16),
                            v_up=_pad_rows(a_v_up[e - 1], 0, LANES).astype(BF16))
            xt, v_first = _even_mix(xt, g_mix, prm, v_first, vres)
        else:
            o = layer // 2
            prm = dict(
                w_in=odd_w_in[o].astype(BF16), w_out=odd_w_out[o].astype(BF16),
                conv_w=c_conv_w[o], conv_b=row(c_conv_b[o]),
                wa=_block_diag(c_wa[o]).astype(BF16), ba=row(c_ba[o]),
                wx=_block_diag(c_wx[o]).astype(BF16), bx=row(c_bx[o]), lam=row(c_lam[o]),
                dw=_block_diag(d_w[o]).astype(BF16), ds=row(d_scale[o]))
            xt = _odd_mix(xt, g_mix, prm)
        last = layer == DEPTH - 1
        xt, w_ffn = _ffn(xt, row(norm_g[layer, 2]), w_ffn, following(2 * layer + 1),
                         final_g=row(final_g) if last else None)
    return xt.reshape(B, T, D)
```

```python
import functools

import jax
import jax.numpy as jnp
from jax import lax
from jax.experimental import pallas as pl
from jax.experimental.pallas import tpu as pltpu

F32 = jnp.float32
BF16 = jnp.bfloat16

D_MODEL = 1024
DEPTH = 4
D_A = 512
HEAD_A = 64
LORA_V = 32
D_IN_A = 1792
D_B = 512
D_C = 512
H_C = 8
D_D = 512
POOL_WINDOWS = (2, 4, 8, 16)
G_D = 128
D_FF = 2816
GN_EPS = 64e-5
RMS_EPS = 1e-6
LRU_C = 8.0
DECAY_SCALE = 0.6065306597126334

LANES = 128
SUBLANES = 8
MXU_DIM = 256
VMEM_LIMIT_BYTES = 56 * 1024 * 1024

TM_FFN = 1024
FF_CHUNK = MXU_DIM
TT = 256
CHUNK = 64
GROUP_W = MXU_DIM
N_GROUPS = D_A // GROUP_W
HALO = SUBLANES
POOL_HALO = 2 * SUBLANES

N_SCORE = 1
N_INV = 1
N_APPLY = 1
N_TRANS = 1
N_STATE = 2
N_SUM = 2

_NN = (((1,), (0,)), ((), ()))
_NT = (((1,), (1,)), ((), ()))
_TN = (((0,), (0,)), ((), ()))


def _params(n_axes=1):
    return pltpu.CompilerParams(dimension_semantics=("arbitrary",) * n_axes,
                                vmem_limit_bytes=VMEM_LIMIT_BYTES)


def _resident(shape):
    zeros = (0,) * len(shape)
    return pl.BlockSpec(shape, lambda i: zeros, pipeline_mode=pl.Buffered(1))


def _rows(width, tile):
    return pl.BlockSpec((tile, width), lambda i: (i, 0))


def _rmsnorm(x, g):
    ms = jnp.mean(x * x, axis=-1, keepdims=True)
    return x * lax.rsqrt(ms + RMS_EPS) * g


def _sigmoid(x):
    return 1.0 / (1.0 + jnp.exp(-x))


def _softplus(z):
    return jnp.maximum(z, 0.0) + jnp.log1p(jnp.exp(-jnp.abs(z)))


def _bdot(x, w):
    return jnp.dot(x.astype(BF16), w, preferred_element_type=F32)


def _split(x, n):
    pieces = []
    for _ in range(n - 1):
        p = x.astype(BF16)
        pieces.append(p)
        x = x - p.astype(F32)
    pieces.append(x.astype(BF16))
    return pieces


def _mm_pieces(xs, ys, dims=_NN):
    n = max(len(xs), len(ys))
    axis = 1 if dims == _TN else 0
    rows = xs[0].shape[axis]
    out = None
    for j, y in enumerate(ys):
        lhs = xs[:n - j]
        if not lhs:
            continue
        stacked = lhs[0] if len(lhs) == 1 else jnp.concatenate(lhs, axis=axis)
        prod = lax.dot_general(stacked, y, dims, preferred_element_type=F32)
        for i in range(len(lhs)):
            part = prod[i * rows:(i + 1) * rows]
            out = part if out is None else out + part
    return out


def _ffn_kernel(*refs, final_norm, cast_next):
    x_ref, g_ref, wg_ref, wu_ref, wd_ref, fg_ref = refs[:6]
    if cast_next:
        o_ref = refs[9]
        for src, dst in zip(refs[6:9], refs[10:13]):
            dst[...] = src[...].astype(BF16)
    else:
        o_ref = refs[6]
    act_ref = refs[-1]
    x = x_ref[...]
    h = _rmsnorm(x, g_ref[...]).astype(BF16)
    for j in range(D_FF // FF_CHUNK):
        cols = slice(j * FF_CHUNK, (j + 1) * FF_CHUNK)
        gate = jnp.dot(h, wg_ref[:, cols], preferred_element_type=F32)
        up = jnp.dot(h, wu_ref[:, cols], preferred_element_type=F32)
        act_ref[:, cols] = (gate * _sigmoid(gate) * up).astype(BF16)
    y = x + 0.5 * jnp.dot(act_ref[...], wd_ref[...], preferred_element_type=F32)
    if final_norm:
        y = _rmsnorm(y, fg_ref[...])
    o_ref[...] = y


def _ffn(x, g, weights, next_f32=None, final_g=None):
    T = x.shape[0]
    n_steps = T // TM_FFN
    fg = g if final_g is None else final_g
    in_specs = [_rows(D_MODEL, TM_FFN), _resident((1, D_MODEL)), _resident((D_MODEL, D_FF)),
                _resident((D_MODEL, D_FF)), _resident((D_FF, D_MODEL)), _resident((1, D_MODEL))]
    args = [x, g, *weights, fg]
    out_specs = [_rows(D_MODEL, TM_FFN)]
    out_shape = [jax.ShapeDtypeStruct((T, D_MODEL), F32)]
    if next_f32 is not None:
        *w_next, layer, k = next_f32
        for w in w_next:
            rows, cols = w.shape[2] // n_steps, w.shape[3]
            in_specs.append(pl.BlockSpec((None, None, rows, cols), lambda i: (layer, k, i, 0)))
            out_specs.append(pl.BlockSpec((rows, cols), lambda i: (i, 0)))
            out_shape.append(jax.ShapeDtypeStruct(w.shape[2:], BF16))
        args += w_next
    outs = pl.pallas_call(
        functools.partial(_ffn_kernel, final_norm=final_g is not None, cast_next=next_f32 is not None),
        grid=(n_steps,),
        in_specs=in_specs,
        out_specs=out_specs,
        out_shape=out_shape,
        scratch_shapes=[pltpu.VMEM((TM_FFN, D_FF), BF16)],
        compiler_params=_params(),
        name="ffn",
    )(*args)
    return outs[0], tuple(outs[1:])


def _rwkv_chunks(kt_s, rt_s, bt_s, kd_s, be_s, ke_s, v_s, cum_s, y_s, st_ref):
    L = CHUNK
    reps = GROUP_W // L
    row = lax.broadcasted_iota(jnp.int32, (L, GROUP_W), 0)
    col = lax.broadcasted_iota(jnp.int32, (L, GROUP_W), 1) & (HEAD_A - 1)
    r256 = lax.broadcasted_iota(jnp.int32, (GROUP_W, GROUP_W), 0)
    c256 = lax.broadcasted_iota(jnp.int32, (GROUP_W, GROUP_W), 1)
    same_head = (r256 // HEAD_A) == (c256 // HEAD_A)
    eye_ss = jnp.where(row == col, 1.0, 0.0)
    strict = row > col
    incl = row >= col
    level_masks = []
    s = 1
    while s < L:
        level_masks.append(((row ^ col) < 2 * s) & ((row & s) > (col & s)))
        s *= 2

    lane = lax.broadcasted_iota(jnp.int32, (L, LANES), 1)
    half_masks = [jnp.where(lane < HEAD_A, 1.0, 0.0).astype(BF16), jnp.where(lane >= HEAD_A, 1.0, 0.0).astype(BF16)]
    zero_tile = jnp.zeros((L, LANES), BF16)

    def bd(pieces):
        out = []
        for p in pieces:
            blocks = []
            for h in range(reps):
                lt = h * HEAD_A // LANES
                half = p[:, lt * LANES:(lt + 1) * LANES] * half_masks[h % 2]
                blocks.append(jnp.concatenate([half if t == lt else zero_tile for t in range(GROUP_W // LANES)],
                                              axis=1))
            out.append(jnp.concatenate(blocks, axis=0))
        return out

    def per_head(xs, y, n):
        return _mm_pieces(xs, bd(_split(y, n)))

    def fold(full):
        masked = jnp.where(same_head, full, 0.0)
        return sum(masked[h * L:(h + 1) * L] for h in range(reps))

    cgs = [(ci, gi) for ci in range(TT // L) for gi in range(N_GROUPS)]
    sl = {cg: (slice(cg[0] * L, (cg[0] + 1) * L), slice(cg[1] * GROUP_W, (cg[1] + 1) * GROUP_W)) for cg in cgs}
    kt_p, a_ab, a_ak, a_rb, a_rk, t_inv, r_hat, y0, m_ss, c_ss = ({} for _ in range(10))

    def scores():
        for cg in cgs:
            kt_p[cg] = _split(kt_s[sl[cg]], max(N_SCORE, N_APPLY))
            rt_p = _split(rt_s[sl[cg]], N_SCORE)
            lhs_p = [jnp.concatenate([a, b], axis=0) for a, b in zip(kt_p[cg][:N_SCORE], rt_p)]
            s_b = _mm_pieces(lhs_p, bd(_split(bt_s[sl[cg]], N_SCORE)), _NT)
            s_k = _mm_pieces(lhs_p, bd(_split(kd_s[sl[cg]], N_SCORE)), _NT)
            a_ab[cg] = s_b[:L]
            a_ak[cg] = jnp.where(strict, s_k[:L], 0.0)
            a_rb[cg] = jnp.where(incl, s_b[L:], 0.0)
            a_rk[cg] = jnp.where(incl, s_k[L:], 0.0)
            t_inv[cg] = eye_ss - jnp.where(level_masks[0], a_ab[cg], 0.0)

    def inverse_level(m):
        def run():
            t_p = {cg: _split(t_inv[cg], N_INV) for cg in cgs}
            left = {cg: per_head(t_p[cg], jnp.where(m, a_ab[cg], 0.0), N_INV) for cg in cgs}
            for cg in cgs:
                t_inv[cg] = t_inv[cg] - _mm_pieces(_split(left[cg], N_INV), bd(t_p[cg]))
        return run

    def apply():
        t_p = {cg: _split(t_inv[cg], N_APPLY) for cg in cgs}
        v_bd = {cg: bd(_split(v_s[sl[cg]], N_APPLY)) for cg in cgs}
        w_mat = {cg: -_mm_pieces(t_p[cg], bd(kt_p[cg][:N_APPLY])) for cg in cgs}
        akv = {cg: _mm_pieces(_split(a_ak[cg], N_APPLY), v_bd[cg]) for cg in cgs}
        u0 = {cg: -per_head(t_p[cg], akv[cg], N_APPLY) for cg in cgs}
        arb_p = {cg: _split(a_rb[cg], N_APPLY) for cg in cgs}
        for cg in cgs:
            r_hat[cg] = rt_s[sl[cg]] + per_head(arb_p[cg], w_mat[cg], N_APPLY)
            y0[cg] = per_head(arb_p[cg], u0[cg], N_APPLY) + _mm_pieces(_split(a_rk[cg], N_APPLY), v_bd[cg])
        for cg in cgs:
            ci = cg[0]
            p_end = jnp.exp(cum_s[ci * L + L - 1:ci * L + L, sl[cg][1]])
            left = jnp.concatenate([be_s[sl[cg]], ke_s[sl[cg]]], axis=0)
            right = jnp.concatenate([jnp.concatenate([w_mat[cg], u0[cg]], axis=1),
                                     jnp.concatenate([jnp.zeros_like(u0[cg]), v_s[sl[cg]]], axis=1)], axis=0)
            full = _mm_pieces(_split(left, N_TRANS), _split(right, N_TRANS), _TN)
            m_ss[cg] = fold(full[:, :GROUP_W]) + eye_ss * p_end
            c_ss[cg] = fold(full[:, GROUP_W:])

    def state():
        for cg in cgs:
            gi = cg[1]
            st_bd = bd(_split(st_ref[gi], N_STATE))
            both = _mm_pieces([jnp.concatenate([a, b], axis=0)
                               for a, b in zip(_split(r_hat[cg], N_STATE), _split(m_ss[cg], N_STATE))], st_bd)
            y_s[sl[cg]] = both[:L] + y0[cg]
            st_ref[gi] = both[L:] + c_ss[cg]

    return [scores] + [inverse_level(m) for m in level_masks[1:]] + [apply, state]


def _seg_sum(x, seg, n=N_SUM):
    return jnp.concatenate([_mm_pieces(_split(x[:, gi * GROUP_W:(gi + 1) * GROUP_W], n), [seg])
                            for gi in range(N_GROUPS)], axis=1)


def _even_kernel(*refs, has_vres):
    (xc_ref, xp_ref, gn_ref, win_ref, mu_ref, w0_ref, wup_ref, a0_ref, aup_ref, gup_ref, kk_ref, ka_ref, rk_ref,
     lnw_ref, lnb_ref, seg_ref, cw_ref, wo_ref) = refs[:18]
    n_in = 18
    if has_vres:
        vf_ref, v0_ref, vdn_ref, vup_ref = refs[18:22]
        n_in = 22
        o_ref = refs[n_in]
        scratch = refs[n_in + 1:]
    else:
        o_ref, vout_ref = refs[n_in:n_in + 2]
        scratch = refs[n_in + 2:]
    (ext_ref, extc_ref, bonus_s, g_s, yb_s, v_s, kt_s, rt_s, bt_s, kd_s, be_s, ke_s, cum_s, y_s, st_ref) = scratch
    L = CHUNK
    i = pl.program_id(0)
    n_tiles = pl.num_programs(0) - 1

    @pl.when(i == 0)
    def _():
        ext_ref[0:HALO, :] = jnp.zeros((HALO, D_IN_A), F32)
        extc_ref[0:HALO, :] = jnp.zeros((HALO, D_B), F32)
        st_ref[...] = jnp.zeros_like(st_ref)
        for ref in (bonus_s, g_s, yb_s, v_s, kt_s, rt_s, bt_s, kd_s, be_s, ke_s, cum_s):
            ref[...] = jnp.zeros_like(ref)

    late = _rwkv_chunks(kt_s, rt_s, bt_s, kd_s, be_s, ke_s, v_s, cum_s, y_s, st_ref)

    def finish():
        y = y_s[...]
        inv_n = 1.0 / HEAD_A
        mean = _seg_sum(y, seg_ref[...]) * inv_n
        d = y - mean
        var = _seg_sum(d * d, seg_ref[...], 1) * inv_n
        y = d * lax.rsqrt(var + GN_EPS) * lnw_ref[...] + lnb_ref[...]
        ya = (y + bonus_s[...]) * g_s[...]
        mix = _bdot(jnp.concatenate([ya, yb_s[...]], axis=-1), wo_ref[...])
        o_ref[...] = xp_ref[...] + mix

    e = {}

    def project(c0, c1):
        p = jnp.dot(e["h"], win_ref[:, c0:c1], preferred_element_type=F32)
        ext_ref[HALO:HALO + TT, c0:c1] = p
        prev = ext_ref[HALO - 1:HALO - 1 + TT, c0:c1]
        ext_ref[0:HALO, c0:c1] = p[TT - HALO:TT, :]
        return p + (prev - p) * mu_ref[:, c0:c1]

    def early_keys():
        e["h"] = _rmsnorm(xc_ref[...], gn_ref[...]).astype(BF16)
        k = project(D_A, 2 * D_A)
        lx = project(3 * D_A, D_IN_A)
        x_wa = lx[:, :LANES]
        z = w0_ref[...] + _bdot(jnp.tanh(x_wa), wup_ref[...])
        e["wl"] = -DECAY_SCALE * _sigmoid(z)
        a = _sigmoid(a0_ref[...] + _bdot(x_wa, aup_ref[...]))
        e["g"] = _bdot(_sigmoid(lx[:, LANES:]), gup_ref[...])
        kn = k * kk_ref[...]
        e["kn"] = kn * lax.rsqrt(jnp.maximum(_seg_sum(kn * kn, seg_ref[...], 1), 1e-24))
        e["k"] = k * (1.0 + (a - 1.0) * ka_ref[...])
        e["b"] = e["kn"] * a

    def early_decay():
        ti = lax.broadcasted_iota(jnp.int32, (TT, TT), 0)
        tj = lax.broadcasted_iota(jnp.int32, (TT, TT), 1)
        tri = jnp.where((ti // L) == (tj // L), jnp.where(tj <= ti, 1.0, 0.0), 0.0).astype(BF16)
        cum = _mm_pieces([tri], _split(e["wl"], N_SUM))
        to_end = jnp.concatenate([cum[c * L + L - 1:c * L + L, :] - cum[c * L:(c + 1) * L, :]
                                  for c in range(TT // L)], axis=0)
        inv_p = jnp.exp(-cum)
        e_end = jnp.exp(to_end)
        e["cum"] = cum
        e["be"] = e["b"] * e_end
        e["ke"] = e["k"] * e_end
        kt_s[...] = e["kn"] * jnp.exp(cum - e["wl"])
        bt_s[...] = e["b"] * inv_p
        kd_s[...] = e["k"] * inv_p

    def early_values():
        r = project(0, D_A)
        v = project(2 * D_A, 3 * D_A)
        if has_vres:
            low = _bdot(v, vdn_ref[...])
            v = v + (vf_ref[...] - v) * _sigmoid(v0_ref[...] + _bdot(low, vup_ref[...]))
        e["v"] = v
        e["rt"] = r * jnp.exp(e["cum"])
        e["bonus"] = _seg_sum(r * e["k"] * rk_ref[...], seg_ref[...], 1) * v

    def early_conv():
        pb = jnp.dot(e["h"], win_ref[:, D_IN_A:], preferred_element_type=F32)
        ch = pb[:, D_B:2 * D_B] * pb[:, 2 * D_B:]
        extc_ref[HALO:HALO + TT, :] = ch
        conv = (cw_ref[0:1, :] * extc_ref[HALO - 2:HALO - 2 + TT, :]
                + cw_ref[1:2, :] * extc_ref[HALO - 1:HALO - 1 + TT, :]
                + cw_ref[2:3, :] * ch)
        extc_ref[0:HALO, :] = ch[TT - HALO:TT, :]
        e["yb"] = pb[:, :D_B] * conv

    for stage in late:
        stage()
    finish()
    for stage in (early_keys, early_decay, early_values, early_conv):
        stage()
    rt_s[...] = e["rt"]
    v_s[...] = e["v"]
    be_s[...] = e["be"]
    ke_s[...] = e["ke"]
    cum_s[...] = e["cum"]
    bonus_s[...] = e["bonus"]
    g_s[...] = e["g"]
    yb_s[...] = e["yb"]
    if not has_vres:
        @pl.when(i < n_tiles)
        def _():
            vout_ref[...] = e["v"]


def _even_mix(x, g, prm, v_first, vres):
    T = x.shape[0]
    n_tiles = T // TT
    has_vres = vres is not None
    cur = lambda w: pl.BlockSpec((TT, w), lambda i: (jnp.minimum(i, n_tiles - 1), 0))
    prv = lambda w: pl.BlockSpec((TT, w), lambda i: (jnp.maximum(i - 1, 0), 0))
    vec = _resident((1, D_A))
    lora = _resident((LANES, D_A))
    in_specs = [cur(D_MODEL), prv(D_MODEL), _resident((1, D_MODEL)), _resident((D_MODEL, D_IN_A + 3 * D_B)),
                _resident((1, D_IN_A)), vec, lora, vec, lora, lora, vec, vec, vec, vec, vec,
                _resident((GROUP_W, GROUP_W)), _resident(prm["conv_w"].shape), _resident((D_MODEL, D_MODEL))]
    args = [x, x, g, prm["w_in"], prm["mu"], prm["w0"], prm["w_up"], prm["a0"], prm["a_up"], prm["g_up"], prm["k_k"],
            prm["k_a"], prm["r_k"], prm["ln_w"], prm["ln_b"], prm["seg"], prm["conv_w"], prm["w_out"]]
    out_specs = [prv(D_MODEL)]
    out_shape = [jax.ShapeDtypeStruct((T, D_MODEL), F32)]
    if has_vres:
        in_specs += [cur(D_A), vec, _resident((D_A, LANES)), lora]
        args += [v_first, vres["v0"], vres["v_dn"], vres["v_up"]]
    else:
        out_specs.append(cur(D_A))
        out_shape.append(jax.ShapeDtypeStruct((T, D_A), F32))
    tile = pltpu.VMEM((TT, D_A), F32)
    scratch = ([pltpu.VMEM((HALO + TT, D_IN_A), F32), pltpu.VMEM((HALO + TT, D_B), F32)] + [tile] * 12
               + [pltpu.VMEM((N_GROUPS, CHUNK, GROUP_W), F32)])
    outs = pl.pallas_call(
        functools.partial(_even_kernel, has_vres=has_vres),
        grid=(n_tiles + 1,),
        in_specs=in_specs,
        out_specs=out_specs,
        out_shape=out_shape,
        scratch_shapes=scratch,
        compiler_params=_params(),
        name="even_mix",
    )(*args)
    return (outs[0], v_first) if has_vres else (outs[0], outs[1])


def _gelu_tanh(x):
    return x * (0.5 * (1.0 + jnp.tanh(0.7978845608028654 * (x + 0.044715 * (x * x * x)))))


def _linear_scan(a, b, h0):
    n, c = a.shape
    nb = n // SUBLANES
    a = a.reshape(nb, SUBLANES, c)
    b = b.reshape(nb, SUBLANES, c)
    sub = lax.broadcasted_iota(jnp.int32, (nb, SUBLANES, c), 1)
    step = 1
    while step < SUBLANES:
        valid = sub >= step
        a_prev = jnp.where(valid, pltpu.roll(a, step, 1), 1.0)
        b_prev = jnp.where(valid, pltpu.roll(b, step, 1), 0.0)
        b = a * b_prev + b
        a = a * a_prev
        step *= 2
    carry = h0
    blocks = []
    for j in range(nb):
        hj = a[j] * carry + b[j]
        blocks.append(hj)
        carry = hj[SUBLANES - 1:SUBLANES, :]
    return jnp.concatenate(blocks, axis=0)


def _odd_kernel(xc_ref, xp_ref, g_ref, win_ref, cw_ref, cb_ref, wa_ref, ba_ref, wx_ref, bx_ref, lam_ref, dw_ref,
                ds_ref, wo_ref, o_ref, pc_s, pd_s, gact_s, extu_ref, extd_ref, h_ref):
    i = pl.program_id(0)

    @pl.when(i == 0)
    def _():
        extu_ref[0:HALO, :] = jnp.zeros((HALO, D_C), F32)
        extd_ref[0:POOL_HALO, :] = jnp.zeros((POOL_HALO, D_D), F32)
        h_ref[...] = jnp.zeros_like(h_ref)
        pc_s[...] = jnp.zeros_like(pc_s)
        pd_s[...] = jnp.zeros_like(pd_s)

    gact_s[...] = _gelu_tanh(pc_s[:, :D_C])
    extu_ref[HALO:HALO + TT, :] = pc_s[:, D_C:]
    extd_ref[POOL_HALO:POOL_HALO + TT, :] = pd_s[...]

    h = _rmsnorm(xc_ref[...], g_ref[...]).astype(BF16)

    def project(blocks):
        for n in blocks:
            cols = slice(n * MXU_DIM, (n + 1) * MXU_DIM)
            p = jnp.dot(h, win_ref[:, cols], preferred_element_type=F32)
            if n < 2 * D_C // MXU_DIM:
                pc_s[:, cols] = p
            else:
                pd_s[:, n * MXU_DIM - 2 * D_C:(n + 1) * MXU_DIM - 2 * D_C] = p

    t_glob = (i - 1) * TT + lax.broadcasted_iota(jnp.int32, (TT, LANES), 0)

    u_in = extu_ref[HALO:HALO + TT, :]
    u = cb_ref[...] + cw_ref[3:4, :] * u_in
    for j in range(3):
        u = u + cw_ref[j:j + 1, :] * extu_ref[HALO - 3 + j:HALO - 3 + j + TT, :]
    extu_ref[0:HALO, :] = u_in[TT - HALO:TT, :]
    project((0, 1))
    rec = _sigmoid(_bdot(u, wa_ref[...]) + ba_ref[...])
    inp = _sigmoid(_bdot(u, wx_ref[...]) + bx_ref[...])
    log_a = (-LRU_C) * rec * _softplus(-lam_ref[...])
    a = jnp.exp(log_a)
    mult = jnp.sqrt(-jnp.tanh(log_a) * (a * a + 1.0))
    row = lax.broadcasted_iota(jnp.int32, (TT, D_C), 0)
    mult = jnp.where(row + (i - 1) * TT == 0, 1.0, mult)
    project((2, 3))
    hs = _linear_scan(a, mult * inp * u, h_ref[...])
    h_ref[...] = jnp.where(i > 0, hs[TT - 1:TT, :], 0.0)
    yc = gact_s[...] * hs
    project((4, 5))

    parts = []
    for gi, win in enumerate(POOL_WINDOWS):
        e = extd_ref[:, gi * G_D:(gi + 1) * G_D]
        span = 1
        while span < win:
            e = e + pltpu.roll(e, span, 0)
            span *= 2
        n_avail = jnp.clip(t_glob + 1, 1, win).astype(F32)
        parts.append(e[POOL_HALO:, :] / n_avail - extd_ref[POOL_HALO:POOL_HALO + TT, gi * G_D:(gi + 1) * G_D])
    extd_ref[0:POOL_HALO, :] = extd_ref[TT:TT + POOL_HALO, :]
    yd = _bdot(jnp.concatenate(parts, axis=-1), dw_ref[...]) * ds_ref[...]

    mix = _bdot(jnp.concatenate([yc, yd], axis=-1), wo_ref[...])
    o_ref[...] = xp_ref[...] + mix


def _odd_mix(x, g, prm):
    T = x.shape[0]
    n_tiles = T // TT
    cur = pl.BlockSpec((TT, D_MODEL), lambda i: (jnp.minimum(i, n_tiles - 1), 0))
    prv = pl.BlockSpec((TT, D_MODEL), lambda i: (jnp.maximum(i - 1, 0), 0))
    vec = _resident((1, D_C))
    sq = _resident((D_C, D_C))
    return pl.pallas_call(
        _odd_kernel,
        grid=(n_tiles + 1,),
        in_specs=[cur, prv, _resident((1, D_MODEL)), _resident((D_MODEL, 2 * D_C + D_D)),
                  _resident(prm["conv_w"].shape), vec, sq, vec, sq, vec, vec, sq, vec,
                  _resident((D_MODEL, D_MODEL))],
        out_specs=prv,
        out_shape=jax.ShapeDtypeStruct((T, D_MODEL), F32),
        scratch_shapes=[pltpu.VMEM((TT, 2 * D_C), F32), pltpu.VMEM((TT, D_D), F32), pltpu.VMEM((TT, D_C), F32),
                        pltpu.VMEM((HALO + TT, D_C), F32), pltpu.VMEM((POOL_HALO + TT, D_D), F32),
                        pltpu.VMEM((1, D_C), F32)],
        compiler_params=_params(),
        name="odd_mix",
    )(x, x, g, prm["w_in"], prm["conv_w"], prm["conv_b"], prm["wa"], prm["ba"], prm["wx"], prm["bx"], prm["lam"],
      prm["dw"], prm["ds"], prm["w_out"])


def _block_diag(w):
    H, n, m = w.shape
    return jnp.einsum("hij,hg->higj", w, jnp.eye(H, dtype=w.dtype)).reshape(H * n, H * m)


def _pad_rows(w, before, total):
    return jnp.pad(w, ((before, total - before - w.shape[0]), (0, 0)))


def kernel(x, norm_g, ffn_wg, ffn_wu, ffn_wd, even_w_in, even_w_out, a_mu, a_w0, a_w_up, a_a0, a_a_up, a_g_up,
           a_k_k, a_k_a, a_r_k, a_ln_w, a_ln_b, a_v0, a_v_dn, a_v_up, b_conv_w, odd_w_in, odd_w_out, c_conv_w,
           c_conv_b, c_wa, c_ba, c_wx, c_bx, c_lam, d_w, d_scale, final_g):
    B, T, D = x.shape
    assert (B, D) == (1, D_MODEL) and T % TM_FFN == 0 and T % TT == 0
    xt = x.reshape(T, D)
    ffn_f32 = (ffn_wg, ffn_wu, ffn_wd)
    w_ffn = tuple(w[0, 0].astype(BF16) for w in ffn_f32)
    n_ffn = 2 * DEPTH
    following = lambda j: (*ffn_f32, (j + 1) // 2, (j + 1) % 2) if j + 1 < n_ffn else None
    row = lambda v: v.reshape(1, -1)
    lora_w = a_w_up.shape[1]
    seg = _block_diag(jnp.ones((GROUP_W // HEAD_A, HEAD_A, HEAD_A), BF16))
    v_first = None
    for layer in range(DEPTH):
        xt, w_ffn = _ffn(xt, row(norm_g[layer, 0]), w_ffn, following(2 * layer))
        g_mix = row(norm_g[layer, 1])
        if layer % 2 == 0:
            e = layer // 2
            prm = dict(
                w_in=even_w_in[e].astype(BF16), w_out=even_w_out[e].astype(BF16), conv_w=b_conv_w[e],
                mu=row(a_mu[e]), w0=row(a_w0[e]), a0=row(a_a0[e]),
                w_up=_pad_rows(a_w_up[e], 0, LANES).astype(BF16),
                a_up=_pad_rows(a_a_up[e], lora_w, LANES).astype(BF16),
                g_up=a_g_up[e].astype(BF16),
                k_k=row(a_k_k[e]), k_a=row(a_k_a[e]), r_k=row(a_r_k[e]), ln_w=row(a_ln_w[e]), ln_b=row(a_ln_b[e]),
                seg=seg)
            vres = None
            if e > 0:
                vres = dict(v0=row(a_v0[e - 1]),
                            v_dn=jnp.pad(a_v_dn[e - 1], ((0, 0), (0, LANES - LORA_V))).astype(BF16),
                            v_up=_pad_rows(a_v_up[e - 1], 0, LANES).astype(BF16))
            xt, v_first = _even_mix(xt, g_mix, prm, v_first, vres)
        else:
            o = layer // 2
            prm = dict(
                w_in=odd_w_in[o].astype(BF16), w_out=odd_w_out[o].astype(BF16),
                conv_w=c_conv_w[o], conv_b=row(c_conv_b[o]),
                wa=_block_diag(c_wa[o]).astype(BF16), ba=row(c_ba[o]),
                wx=_block_diag(c_wx[o]).astype(BF16), bx=row(c_bx[o]), lam=row(c_lam[o]),
                dw=_block_diag(d_w[o]).astype(BF16), ds=row(d_scale[o]))
            xt = _odd_mix(xt, g_mix, prm)
        last = layer == DEPTH - 1
        xt, w_ffn = _ffn(xt, row(norm_g[layer, 2]), w_ffn, following(2 * layer + 1),
                         final_g=row(final_g) if last else None)
    return xt.reshape(B, T, D)
```

```python
import functools

import jax
import jax.numpy as jnp
from jax import lax
from jax.experimental import pallas as pl
from jax.experimental.pallas import tpu as pltpu

F32 = jnp.float32
BF16 = jnp.bfloat16

D_MODEL = 1024
DEPTH = 4
D_A = 512
HEAD_A = 64
LORA_V = 32
D_IN_A = 1792
D_B = 512
D_C = 512
H_C = 8
D_D = 512
POOL_WINDOWS = (2, 4, 8, 16)
G_D = 128
D_FF = 2816
GN_EPS = 64e-5
RMS_EPS = 1e-6
LRU_C = 8.0
DECAY_SCALE = 0.6065306597126334

LANES = 128
SUBLANES = 8
MXU_DIM = 256
VMEM_LIMIT_BYTES = 56 * 1024 * 1024

TM_FFN = 1024
FF_CHUNK = MXU_DIM
FFN_SUBTILES = 4
FFN_NORM_SLICES = 8
TT = 256
CHUNK = 64
GROUP_W = MXU_DIM
N_GROUPS = D_A // GROUP_W
HALO = SUBLANES
POOL_HALO = 2 * SUBLANES

N_SCORE = 1
N_INV = 1
N_APPLY = 1
N_TRANS = 1
N_STATE = 2
N_SUM = 2

_NN = (((1,), (0,)), ((), ()))
_NT = (((1,), (1,)), ((), ()))
_TN = (((0,), (0,)), ((), ()))


def _params(n_axes=1):
    return pltpu.CompilerParams(dimension_semantics=("arbitrary",) * n_axes,
                                vmem_limit_bytes=VMEM_LIMIT_BYTES)


def _resident(shape):
    zeros = (0,) * len(shape)
    return pl.BlockSpec(shape, lambda i: zeros, pipeline_mode=pl.Buffered(1))


def _rows(width, tile):
    return pl.BlockSpec((tile, width), lambda i: (i, 0))


def _rmsnorm(x, g):
    ms = jnp.mean(x * x, axis=-1, keepdims=True)
    return x * lax.rsqrt(ms + RMS_EPS) * g


def _sigmoid(x):
    return 1.0 / (1.0 + jnp.exp(-x))


def _softplus(z):
    return jnp.maximum(z, 0.0) + jnp.log1p(jnp.exp(-jnp.abs(z)))


def _bdot(x, w):
    return jnp.dot(x.astype(BF16), w, preferred_element_type=F32)


def _split(x, n):
    pieces = []
    for _ in range(n - 1):
        p = x.astype(BF16)
        pieces.append(p)
        x = x - p.astype(F32)
    pieces.append(x.astype(BF16))
    return pieces


def _mm_pieces(xs, ys, dims=_NN):
    n = max(len(xs), len(ys))
    axis = 1 if dims == _TN else 0
    rows = xs[0].shape[axis]
    out = None
    for j, y in enumerate(ys):
        lhs = xs[:n - j]
        if not lhs:
            continue
        stacked = lhs[0] if len(lhs) == 1 else jnp.concatenate(lhs, axis=axis)
        prod = lax.dot_general(stacked, y, dims, preferred_element_type=F32)
        for i in range(len(lhs)):
            part = prod[i * rows:(i + 1) * rows]
            out = part if out is None else out + part
    return out


def _ffn_kernel(*refs, final_norm, cast_next):
    x_ref, g_ref, wg_ref, wu_ref, wd_ref, fg_ref = refs[:6]
    if cast_next:
        o_ref = refs[9]
        for src, dst in zip(refs[6:9], refs[10:13]):
            dst[...] = src[...].astype(BF16)
    else:
        o_ref = refs[6]
    h_ref, act_ref = refs[-2:]
    sub = TM_FFN // FFN_SUBTILES
    n_chunks = D_FF // FF_CHUNK
    step = sub // FFN_NORM_SLICES

    def normalize(r0, r1):
        h_ref[r0:r1, :] = _rmsnorm(x_ref[r0:r1, :], g_ref[...]).astype(BF16)

    normalize(0, sub)
    for s in range(FFN_SUBTILES):
        rows = slice(s * sub, (s + 1) * sub)
        h = h_ref[rows, :]
        for j in range(n_chunks):
            cols = slice(j * FF_CHUNK, (j + 1) * FF_CHUNK)
            gate = jnp.dot(h, wg_ref[:, cols], preferred_element_type=F32)
            up = jnp.dot(h, wu_ref[:, cols], preferred_element_type=F32)
            act_ref[rows, cols] = (gate * _sigmoid(gate) * up).astype(BF16)
            if s + 1 < FFN_SUBTILES and j < FFN_NORM_SLICES:
                normalize((s + 1) * sub + j * step, (s + 1) * sub + (j + 1) * step)
        y = x_ref[rows, :] + 0.5 * jnp.dot(act_ref[rows, :], wd_ref[...], preferred_element_type=F32)
        if final_norm:
            y = _rmsnorm(y, fg_ref[...])
        o_ref[rows, :] = y


def _ffn(x, g, weights, next_f32=None, final_g=None):
    T = x.shape[0]
    n_steps = T // TM_FFN
    fg = g if final_g is None else final_g
    in_specs = [_rows(D_MODEL, TM_FFN), _resident((1, D_MODEL)), _resident((D_MODEL, D_FF)),
                _resident((D_MODEL, D_FF)), _resident((D_FF, D_MODEL)), _resident((1, D_MODEL))]
    args = [x, g, *weights, fg]
    out_specs = [_rows(D_MODEL, TM_FFN)]
    out_shape = [jax.ShapeDtypeStruct((T, D_MODEL), F32)]
    if next_f32 is not None:
        *w_next, layer, k = next_f32
        for w in w_next:
            rows, cols = w.shape[2] // n_steps, w.shape[3]
            in_specs.append(pl.BlockSpec((None, None, rows, cols), lambda i: (layer, k, i, 0)))
            out_specs.append(pl.BlockSpec((rows, cols), lambda i: (i, 0)))
            out_shape.append(jax.ShapeDtypeStruct(w.shape[2:], BF16))
        args += w_next
    outs = pl.pallas_call(
        functools.partial(_ffn_kernel, final_norm=final_g is not None, cast_next=next_f32 is not None),
        grid=(n_steps,),
        in_specs=in_specs,
        out_specs=out_specs,
        out_shape=out_shape,
        scratch_shapes=[pltpu.VMEM((TM_FFN, D_MODEL), BF16), pltpu.VMEM((TM_FFN, D_FF), BF16)],
        compiler_params=_params(),
        name="ffn",
    )(*args)
    return outs[0], tuple(outs[1:])


def _rwkv_chunks(kt_s, rt_s, bt_s, kd_s, be_s, ke_s, v_s, cum_s, y_s, st_ref):
    L = CHUNK
    reps = GROUP_W // L
    row = lax.broadcasted_iota(jnp.int32, (L, GROUP_W), 0)
    col = lax.broadcasted_iota(jnp.int32, (L, GROUP_W), 1) & (HEAD_A - 1)
    r256 = lax.broadcasted_iota(jnp.int32, (GROUP_W, GROUP_W), 0)
    c256 = lax.broadcasted_iota(jnp.int32, (GROUP_W, GROUP_W), 1)
    same_head = (r256 // HEAD_A) == (c256 // HEAD_A)
    eye_ss = jnp.where(row == col, 1.0, 0.0)
    strict = row > col
    incl = row >= col
    level_masks = []
    s = 1
    while s < L:
        level_masks.append(((row ^ col) < 2 * s) & ((row & s) > (col & s)))
        s *= 2

    lane = lax.broadcasted_iota(jnp.int32, (L, LANES), 1)
    half_masks = [jnp.where(lane < HEAD_A, 1.0, 0.0).astype(BF16), jnp.where(lane >= HEAD_A, 1.0, 0.0).astype(BF16)]
    zero_tile = jnp.zeros((L, LANES), BF16)

    def bd(pieces):
        out = []
        for p in pieces:
            blocks = []
            for h in range(reps):
                lt = h * HEAD_A // LANES
                half = p[:, lt * LANES:(lt + 1) * LANES] * half_masks[h % 2]
                blocks.append(jnp.concatenate([half if t == lt else zero_tile for t in range(GROUP_W // LANES)],
                                              axis=1))
            out.append(jnp.concatenate(blocks, axis=0))
        return out

    def per_head(xs, y, n):
        return _mm_pieces(xs, bd(_split(y, n)))

    def fold(full):
        masked = jnp.where(same_head, full, 0.0)
        return sum(masked[h * L:(h + 1) * L] for h in range(reps))

    cgs = [(ci, gi) for ci in range(TT // L) for gi in range(N_GROUPS)]
    sl = {cg: (slice(cg[0] * L, (cg[0] + 1) * L), slice(cg[1] * GROUP_W, (cg[1] + 1) * GROUP_W)) for cg in cgs}
    kt_p, a_ab, a_ak, a_rb, a_rk, t_inv, r_hat, y0, m_ss, c_ss = ({} for _ in range(10))

    def scores():
        for cg in cgs:
            kt_p[cg] = _split(kt_s[sl[cg]], max(N_SCORE, N_APPLY))
            rt_p = _split(rt_s[sl[cg]], N_SCORE)
            lhs_p = [jnp.concatenate([a, b], axis=0) for a, b in zip(kt_p[cg][:N_SCORE], rt_p)]
            s_b = _mm_pieces(lhs_p, bd(_split(bt_s[sl[cg]], N_SCORE)), _NT)
            s_k = _mm_pieces(lhs_p, bd(_split(kd_s[sl[cg]], N_SCORE)), _NT)
            a_ab[cg] = s_b[:L]
            a_ak[cg] = jnp.where(strict, s_k[:L], 0.0)
            a_rb[cg] = jnp.where(incl, s_b[L:], 0.0)
            a_rk[cg] = jnp.where(incl, s_k[L:], 0.0)
            t_inv[cg] = eye_ss - jnp.where(level_masks[0], a_ab[cg], 0.0)

    def inverse_level(m):
        def run():
            t_p = {cg: _split(t_inv[cg], N_INV) for cg in cgs}
            left = {cg: per_head(t_p[cg], jnp.where(m, a_ab[cg], 0.0), N_INV) for cg in cgs}
            for cg in cgs:
                t_inv[cg] = t_inv[cg] - _mm_pieces(_split(left[cg], N_INV), bd(t_p[cg]))
        return run

    def apply():
        t_p = {cg: _split(t_inv[cg], N_APPLY) for cg in cgs}
        v_bd = {cg: bd(_split(v_s[sl[cg]], N_APPLY)) for cg in cgs}
        w_mat = {cg: -_mm_pieces(t_p[cg], bd(kt_p[cg][:N_APPLY])) for cg in cgs}
        akv = {cg: _mm_pieces(_split(a_ak[cg], N_APPLY), v_bd[cg]) for cg in cgs}
        u0 = {cg: -per_head(t_p[cg], akv[cg], N_APPLY) for cg in cgs}
        arb_p = {cg: _split(a_rb[cg], N_APPLY) for cg in cgs}
        for cg in cgs:
            r_hat[cg] = rt_s[sl[cg]] + per_head(arb_p[cg], w_mat[cg], N_APPLY)
            y0[cg] = per_head(arb_p[cg], u0[cg], N_APPLY) + _mm_pieces(_split(a_rk[cg], N_APPLY), v_bd[cg])
        for cg in cgs:
            ci = cg[0]
            p_end = jnp.exp(cum_s[ci * L + L - 1:ci * L + L, sl[cg][1]])
            left = jnp.concatenate([be_s[sl[cg]], ke_s[sl[cg]]], axis=0)
            right = jnp.concatenate([jnp.concatenate([w_mat[cg], u0[cg]], axis=1),
                                     jnp.concatenate([jnp.zeros_like(u0[cg]), v_s[sl[cg]]], axis=1)], axis=0)
            full = _mm_pieces(_split(left, N_TRANS), _split(right, N_TRANS), _TN)
            m_ss[cg] = fold(full[:, :GROUP_W]) + eye_ss * p_end
            c_ss[cg] = fold(full[:, GROUP_W:])

    def state():
        for cg in cgs:
            gi = cg[1]
            st_bd = bd(_split(st_ref[gi], N_STATE))
            both = _mm_pieces([jnp.concatenate([a, b], axis=0)
                               for a, b in zip(_split(r_hat[cg], N_STATE), _split(m_ss[cg], N_STATE))], st_bd)
            y_s[sl[cg]] = both[:L] + y0[cg]
            st_ref[gi] = both[L:] + c_ss[cg]

    return [scores] + [inverse_level(m) for m in level_masks[1:]] + [apply, state]


def _seg_sum(x, seg, n=N_SUM):
    return jnp.concatenate([_mm_pieces(_split(x[:, gi * GROUP_W:(gi + 1) * GROUP_W], n), [seg])
                            for gi in range(N_GROUPS)], axis=1)


def _even_kernel(*refs, has_vres):
    (xc_ref, xp_ref, gn_ref, win_ref, mu_ref, w0_ref, wup_ref, a0_ref, aup_ref, gup_ref, kk_ref, ka_ref, rk_ref,
     lnw_ref, lnb_ref, seg_ref, cw_ref, wo_ref) = refs[:18]
    n_in = 18
    if has_vres:
        vf_ref, v0_ref, vdn_ref, vup_ref = refs[18:22]
        n_in = 22
        o_ref = refs[n_in]
        scratch = refs[n_in + 1:]
    else:
        o_ref, vout_ref = refs[n_in:n_in + 2]
        scratch = refs[n_in + 2:]
    (ext_ref, extc_ref, bonus_s, g_s, yb_s, v_s, kt_s, rt_s, bt_s, kd_s, be_s, ke_s, cum_s, y_s, st_ref) = scratch
    L = CHUNK
    i = pl.program_id(0)
    n_tiles = pl.num_programs(0) - 1

    @pl.when(i == 0)
    def _():
        ext_ref[0:HALO, :] = jnp.zeros((HALO, D_IN_A), F32)
        extc_ref[0:HALO, :] = jnp.zeros((HALO, D_B), F32)
        st_ref[...] = jnp.zeros_like(st_ref)
        for ref in (bonus_s, g_s, yb_s, v_s, kt_s, rt_s, bt_s, kd_s, be_s, ke_s, cum_s):
            ref[...] = jnp.zeros_like(ref)

    late = _rwkv_chunks(kt_s, rt_s, bt_s, kd_s, be_s, ke_s, v_s, cum_s, y_s, st_ref)

    def finish():
        y = y_s[...]
        inv_n = 1.0 / HEAD_A
        mean = _seg_sum(y, seg_ref[...]) * inv_n
        d = y - mean
        var = _seg_sum(d * d, seg_ref[...], 1) * inv_n
        y = d * lax.rsqrt(var + GN_EPS) * lnw_ref[...] + lnb_ref[...]
        ya = (y + bonus_s[...]) * g_s[...]
        mix = _bdot(jnp.concatenate([ya, yb_s[...]], axis=-1), wo_ref[...])
        o_ref[...] = xp_ref[...] + mix

    e = {}

    def project(c0, c1):
        p = jnp.dot(e["h"], win_ref[:, c0:c1], preferred_element_type=F32)
        ext_ref[HALO:HALO + TT, c0:c1] = p
        prev = ext_ref[HALO - 1:HALO - 1 + TT, c0:c1]
        ext_ref[0:HALO, c0:c1] = p[TT - HALO:TT, :]
        return p + (prev - p) * mu_ref[:, c0:c1]

    def early_keys():
        e["h"] = _rmsnorm(xc_ref[...], gn_ref[...]).astype(BF16)
        k = project(D_A, 2 * D_A)
        lx = project(3 * D_A, D_IN_A)
        x_wa = lx[:, :LANES]
        z = w0_ref[...] + _bdot(jnp.tanh(x_wa), wup_ref[...])
        e["wl"] = -DECAY_SCALE * _sigmoid(z)
        a = _sigmoid(a0_ref[...] + _bdot(x_wa, aup_ref[...]))
        e["g"] = _bdot(_sigmoid(lx[:, LANES:]), gup_ref[...])
        kn = k * kk_ref[...]
        e["kn"] = kn * lax.rsqrt(jnp.maximum(_seg_sum(kn * kn, seg_ref[...], 1), 1e-24))
        e["k"] = k * (1.0 + (a - 1.0) * ka_ref[...])
        e["b"] = e["kn"] * a

    def early_decay():
        ti = lax.broadcasted_iota(jnp.int32, (TT, TT), 0)
        tj = lax.broadcasted_iota(jnp.int32, (TT, TT), 1)
        tri = jnp.where((ti // L) == (tj // L), jnp.where(tj <= ti, 1.0, 0.0), 0.0).astype(BF16)
        cum = _mm_pieces([tri], _split(e["wl"], N_SUM))
        to_end = jnp.concatenate([cum[c * L + L - 1:c * L + L, :] - cum[c * L:(c + 1) * L, :]
                                  for c in range(TT // L)], axis=0)
        inv_p = jnp.exp(-cum)
        e_end = jnp.exp(to_end)
        e["cum"] = cum
        e["be"] = e["b"] * e_end
        e["ke"] = e["k"] * e_end
        kt_s[...] = e["kn"] * jnp.exp(cum - e["wl"])
        bt_s[...] = e["b"] * inv_p
        kd_s[...] = e["k"] * inv_p

    def early_values():
        r = project(0, D_A)
        v = project(2 * D_A, 3 * D_A)
        if has_vres:
            low = _bdot(v, vdn_ref[...])
            v = v + (vf_ref[...] - v) * _sigmoid(v0_ref[...] + _bdot(low, vup_ref[...]))
        e["v"] = v
        e["rt"] = r * jnp.exp(e["cum"])
        e["bonus"] = _seg_sum(r * e["k"] * rk_ref[...], seg_ref[...], 1) * v

    def early_conv():
        pb = jnp.dot(e["h"], win_ref[:, D_IN_A:], preferred_element_type=F32)
        ch = pb[:, D_B:2 * D_B] * pb[:, 2 * D_B:]
        extc_ref[HALO:HALO + TT, :] = ch
        conv = (cw_ref[0:1, :] * extc_ref[HALO - 2:HALO - 2 + TT, :]
                + cw_ref[1:2, :] * extc_ref[HALO - 1:HALO - 1 + TT, :]
                + cw_ref[2:3, :] * ch)
        extc_ref[0:HALO, :] = ch[TT - HALO:TT, :]
        e["yb"] = pb[:, :D_B] * conv

    for stage in late:
        stage()
    finish()
    for stage in (early_keys, early_decay, early_values, early_conv):
        stage()
    rt_s[...] = e["rt"]
    v_s[...] = e["v"]
    be_s[...] = e["be"]
    ke_s[...] = e["ke"]
    cum_s[...] = e["cum"]
    bonus_s[...] = e["bonus"]
    g_s[...] = e["g"]
    yb_s[...] = e["yb"]
    if not has_vres:
        @pl.when(i < n_tiles)
        def _():
            vout_ref[...] = e["v"]


def _even_mix(x, g, prm, v_first, vres):
    T = x.shape[0]
    n_tiles = T // TT
    has_vres = vres is not None
    cur = lambda w: pl.BlockSpec((TT, w), lambda i: (jnp.minimum(i, n_tiles - 1), 0))
    prv = lambda w: pl.BlockSpec((TT, w), lambda i: (jnp.maximum(i - 1, 0), 0))
    vec = _resident((1, D_A))
    lora = _resident((LANES, D_A))
    in_specs = [cur(D_MODEL), prv(D_MODEL), _resident((1, D_MODEL)), _resident((D_MODEL, D_IN_A + 3 * D_B)),
                _resident((1, D_IN_A)), vec, lora, vec, lora, lora, vec, vec, vec, vec, vec,
                _resident((GROUP_W, GROUP_W)), _resident(prm["conv_w"].shape), _resident((D_MODEL, D_MODEL))]
    args = [x, x, g, prm["w_in"], prm["mu"], prm["w0"], prm["w_up"], prm["a0"], prm["a_up"], prm["g_up"], prm["k_k"],
            prm["k_a"], prm["r_k"], prm["ln_w"], prm["ln_b"], prm["seg"], prm["conv_w"], prm["w_out"]]
    out_specs = [prv(D_MODEL)]
    out_shape = [jax.ShapeDtypeStruct((T, D_MODEL), F32)]
    if has_vres:
        in_specs += [cur(D_A), vec, _resident((D_A, LANES)), lora]
        args += [v_first, vres["v0"], vres["v_dn"], vres["v_up"]]
    else:
        out_specs.append(cur(D_A))
        out_shape.append(jax.ShapeDtypeStruct((T, D_A), F32))
    tile = pltpu.VMEM((TT, D_A), F32)
    scratch = ([pltpu.VMEM((HALO + TT, D_IN_A), F32), pltpu.VMEM((HALO + TT, D_B), F32)] + [tile] * 12
               + [pltpu.VMEM((N_GROUPS, CHUNK, GROUP_W), F32)])
    outs = pl.pallas_call(
        functools.partial(_even_kernel, has_vres=has_vres),
        grid=(n_tiles + 1,),
        in_specs=in_specs,
        out_specs=out_specs,
        out_shape=out_shape,
        scratch_shapes=scratch,
        compiler_params=_params(),
        name="even_mix",
    )(*args)
    return (outs[0], v_first) if has_vres else (outs[0], outs[1])


def _gelu_tanh(x):
    return x * (0.5 * (1.0 + jnp.tanh(0.7978845608028654 * (x + 0.044715 * (x * x * x)))))


def _linear_scan(a, b, h0):
    n, c = a.shape
    nb = n // SUBLANES
    a = a.reshape(nb, SUBLANES, c)
    b = b.reshape(nb, SUBLANES, c)
    sub = lax.broadcasted_iota(jnp.int32, (nb, SUBLANES, c), 1)
    step = 1
    while step < SUBLANES:
        valid = sub >= step
        a_prev = jnp.where(valid, pltpu.roll(a, step, 1), 1.0)
        b_prev = jnp.where(valid, pltpu.roll(b, step, 1), 0.0)
        b = a * b_prev + b
        a = a * a_prev
        step *= 2
    carry = h0
    blocks = []
    for j in range(nb):
        hj = a[j] * carry + b[j]
        blocks.append(hj)
        carry = hj[SUBLANES - 1:SUBLANES, :]
    return jnp.concatenate(blocks, axis=0)


def _odd_kernel(xc_ref, xp_ref, g_ref, win_ref, cw_ref, cb_ref, wa_ref, ba_ref, wx_ref, bx_ref, lam_ref, dw_ref,
                ds_ref, wo_ref, o_ref, pc_s, pd_s, gact_s, extu_ref, extd_ref, h_ref):
    i = pl.program_id(0)

    @pl.when(i == 0)
    def _():
        extu_ref[0:HALO, :] = jnp.zeros((HALO, D_C), F32)
        extd_ref[0:POOL_HALO, :] = jnp.zeros((POOL_HALO, D_D), F32)
        h_ref[...] = jnp.zeros_like(h_ref)
        pc_s[...] = jnp.zeros_like(pc_s)
        pd_s[...] = jnp.zeros_like(pd_s)

    gact_s[...] = _gelu_tanh(pc_s[:, :D_C])
    extu_ref[HALO:HALO + TT, :] = pc_s[:, D_C:]
    extd_ref[POOL_HALO:POOL_HALO + TT, :] = pd_s[...]

    h = _rmsnorm(xc_ref[...], g_ref[...]).astype(BF16)

    def project(blocks):
        for n in blocks:
            cols = slice(n * MXU_DIM, (n + 1) * MXU_DIM)
            p = jnp.dot(h, win_ref[:, cols], preferred_element_type=F32)
            if n < 2 * D_C // MXU_DIM:
                pc_s[:, cols] = p
            else:
                pd_s[:, n * MXU_DIM - 2 * D_C:(n + 1) * MXU_DIM - 2 * D_C] = p

    t_glob = (i - 1) * TT + lax.broadcasted_iota(jnp.int32, (TT, LANES), 0)

    u_in = extu_ref[HALO:HALO + TT, :]
    u = cb_ref[...] + cw_ref[3:4, :] * u_in
    for j in range(3):
        u = u + cw_ref[j:j + 1, :] * extu_ref[HALO - 3 + j:HALO - 3 + j + TT, :]
    extu_ref[0:HALO, :] = u_in[TT - HALO:TT, :]
    project((0, 1))
    rec = _sigmoid(_bdot(u, wa_ref[...]) + ba_ref[...])
    inp = _sigmoid(_bdot(u, wx_ref[...]) + bx_ref[...])
    log_a = (-LRU_C) * rec * _softplus(-lam_ref[...])
    a = jnp.exp(log_a)
    mult = jnp.sqrt(-jnp.tanh(log_a) * (a * a + 1.0))
    row = lax.broadcasted_iota(jnp.int32, (TT, D_C), 0)
    mult = jnp.where(row + (i - 1) * TT == 0, 1.0, mult)
    project((2, 3))
    hs = _linear_scan(a, mult * inp * u, h_ref[...])
    h_ref[...] = jnp.where(i > 0, hs[TT - 1:TT, :], 0.0)
    yc = gact_s[...] * hs
    project((4, 5))

    parts = []
    for gi, win in enumerate(POOL_WINDOWS):
        e = extd_ref[:, gi * G_D:(gi + 1) * G_D]
        span = 1
        while span < win:
            e = e + pltpu.roll(e, span, 0)
            span *= 2
        n_avail = jnp.clip(t_glob + 1, 1, win).astype(F32)
        parts.append(e[POOL_HALO:, :] / n_avail - extd_ref[POOL_HALO:POOL_HALO + TT, gi * G_D:(gi + 1) * G_D])
    extd_ref[0:POOL_HALO, :] = extd_ref[TT:TT + POOL_HALO, :]
    yd = _bdot(jnp.concatenate(parts, axis=-1), dw_ref[...]) * ds_ref[...]

    mix = _bdot(jnp.concatenate([yc, yd], axis=-1), wo_ref[...])
    o_ref[...] = xp_ref[...] + mix


def _odd_mix(x, g, prm):
    T = x.shape[0]
    n_tiles = T // TT
    cur = pl.BlockSpec((TT, D_MODEL), lambda i: (jnp.minimum(i, n_tiles - 1), 0))
    prv = pl.BlockSpec((TT, D_MODEL), lambda i: (jnp.maximum(i - 1, 0), 0))
    vec = _resident((1, D_C))
    sq = _resident((D_C, D_C))
    return pl.pallas_call(
        _odd_kernel,
        grid=(n_tiles + 1,),
        in_specs=[cur, prv, _resident((1, D_MODEL)), _resident((D_MODEL, 2 * D_C + D_D)),
                  _resident(prm["conv_w"].shape), vec, sq, vec, sq, vec, vec, sq, vec,
                  _resident((D_MODEL, D_MODEL))],
        out_specs=prv,
        out_shape=jax.ShapeDtypeStruct((T, D_MODEL), F32),
        scratch_shapes=[pltpu.VMEM((TT, 2 * D_C), F32), pltpu.VMEM((TT, D_D), F32), pltpu.VMEM((TT, D_C), F32),
                        pltpu.VMEM((HALO + TT, D_C), F32), pltpu.VMEM((POOL_HALO + TT, D_D), F32),
                        pltpu.VMEM((1, D_C), F32)],
        compiler_params=_params(),
        name="odd_mix",
    )(x, x, g, prm["w_in"], prm["conv_w"], prm["conv_b"], prm["wa"], prm["ba"], prm["wx"], prm["bx"], prm["lam"],
      prm["dw"], prm["ds"], prm["w_out"])


def _block_diag(w):
    H, n, m = w.shape
    return jnp.einsum("hij,hg->higj", w, jnp.eye(H, dtype=w.dtype)).reshape(H * n, H * m)


def _pad_rows(w, before, total):
    return jnp.pad(w, ((before, total - before - w.shape[0]), (0, 0)))


def kernel(x, norm_g, ffn_wg, ffn_wu, ffn_wd, even_w_in, even_w_out, a_mu, a_w0, a_w_up, a_a0, a_a_up, a_g_up,
           a_k_k, a_k_a, a_r_k, a_ln_w, a_ln_b, a_v0, a_v_dn, a_v_up, b_conv_w, odd_w_in, odd_w_out, c_conv_w,
           c_conv_b, c_wa, c_ba, c_wx, c_bx, c_lam, d_w, d_scale, final_g):
    B, T, D = x.shape
    assert (B, D) == (1, D_MODEL) and T % TM_FFN == 0 and T % TT == 0
    xt = x.reshape(T, D)
    ffn_f32 = (ffn_wg, ffn_wu, ffn_wd)
    w_ffn = tuple(w[0, 0].astype(BF16) for w in ffn_f32)
    n_ffn = 2 * DEPTH
    following = lambda j: (*ffn_f32, (j + 1) // 2, (j + 1) % 2) if j + 1 < n_ffn else None
    row = lambda v: v.reshape(1, -1)
    lora_w = a_w_up.shape[1]
    seg = _block_diag(jnp.ones((GROUP_W // HEAD_A, HEAD_A, HEAD_A), BF16))
    v_first = None
    for layer in range(DEPTH):
        xt, w_ffn = _ffn(xt, row(norm_g[layer, 0]), w_ffn, following(2 * layer))
        g_mix = row(norm_g[layer, 1])
        if layer % 2 == 0:
            e = layer // 2
            prm = dict(
                w_in=even_w_in[e].astype(BF16), w_out=even_w_out[e].astype(BF16), conv_w=b_conv_w[e],
                mu=row(a_mu[e]), w0=row(a_w0[e]), a0=row(a_a0[e]),
                w_up=_pad_rows(a_w_up[e], 0, LANES).astype(BF16),
                a_up=_pad_rows(a_a_up[e], lora_w, LANES).astype(BF16),
                g_up=a_g_up[e].astype(BF16),
                k_k=row(a_k_k[e]), k_a=row(a_k_a[e]), r_k=row(a_r_k[e]), ln_w=row(a_ln_w[e]), ln_b=row(a_ln_b[e]),
                seg=seg)
            vres = None
            if e > 0:
                vres = dict(v0=row(a_v0[e - 1]),
                            v_dn=jnp.pad(a_v_dn[e - 1], ((0, 0), (0, LANES - LORA_V))).astype(BF16),
                            v_up=_pad_rows(a_v_up[e - 1], 0, LANES).astype(BF16))
            xt, v_first = _even_mix(xt, g_mix, prm, v_first, vres)
        else:
            o = layer // 2
            prm = dict(
                w_in=odd_w_in[o].astype(BF16), w_out=odd_w_out[o].astype(BF16),
                conv_w=c_conv_w[o], conv_b=row(c_conv_b[o]),
                wa=_block_diag(c_wa[o]).astype(BF16), ba=row(c_ba[o]),
                wx=_block_diag(c_wx[o]).astype(BF16), bx=row(c_bx[o]), lam=row(c_lam[o]),
                dw=_block_diag(d_w[o]).astype(BF16), ds=row(d_scale[o]))
            xt = _odd_mix(xt, g_mix, prm)
        last = layer == DEPTH - 1
        xt, w_ffn = _ffn(xt, row(norm_g[layer, 2]), w_ffn, following(2 * layer + 1),
                         final_g=row(final_g) if last else None)
    return xt.reshape(B, T, D)
```

```python
import functools

import jax
import jax.numpy as jnp
from jax import lax
from jax.experimental import pallas as pl
from jax.experimental.pallas import tpu as pltpu

F32 = jnp.float32
BF16 = jnp.bfloat16

D_MODEL = 1024
DEPTH = 4
D_A = 512
HEAD_A = 64
LORA_V = 32
D_IN_A = 1792
D_B = 512
D_C = 512
H_C = 8
D_D = 512
POOL_WINDOWS = (2, 4, 8, 16)
G_D = 128
D_FF = 2816
GN_EPS = 64e-5
RMS_EPS = 1e-6
LRU_C = 8.0
DECAY_SCALE = 0.6065306597126334

LANES = 128
SUBLANES = 8
MXU_DIM = 256
VMEM_LIMIT_BYTES = 56 * 1024 * 1024

TM_FFN = 1024
FF_CHUNK = MXU_DIM
FFN_SUBTILES = 4
FFN_NORM_SLICES = 8
TT = 256
CHUNK = 64
GROUP_W = MXU_DIM
N_GROUPS = D_A // GROUP_W
HALO = SUBLANES
POOL_HALO = 2 * SUBLANES
SCAN_SEG = TT // SUBLANES
SCAN_PITCH = SCAN_SEG + SUBLANES

N_SCORE = 1
N_INV = 1
N_APPLY = 1
N_TRANS = 1
N_STATE = 2
N_SUM = 2

_NN = (((1,), (0,)), ((), ()))
_NT = (((1,), (1,)), ((), ()))
_TN = (((0,), (0,)), ((), ()))


def _params(n_axes=1):
    return pltpu.CompilerParams(dimension_semantics=("arbitrary",) * n_axes,
                                vmem_limit_bytes=VMEM_LIMIT_BYTES)


def _resident(shape):
    zeros = (0,) * len(shape)
    return pl.BlockSpec(shape, lambda i: zeros, pipeline_mode=pl.Buffered(1))


def _rows(width, tile):
    return pl.BlockSpec((tile, width), lambda i: (i, 0))


def _rmsnorm(x, g):
    ms = jnp.mean(x * x, axis=-1, keepdims=True)
    return x * lax.rsqrt(ms + RMS_EPS) * g


def _sigmoid(x):
    return 1.0 / (1.0 + jnp.exp(-x))


def _softplus(z):
    return jnp.maximum(z, 0.0) + jnp.log1p(jnp.exp(-jnp.abs(z)))


def _bdot(x, w):
    return jnp.dot(x.astype(BF16), w, preferred_element_type=F32)


def _split(x, n):
    pieces = []
    for _ in range(n - 1):
        p = x.astype(BF16)
        pieces.append(p)
        x = x - p.astype(F32)
    pieces.append(x.astype(BF16))
    return pieces


def _mm_pieces(xs, ys, dims=_NN):
    n = max(len(xs), len(ys))
    axis = 1 if dims == _TN else 0
    rows = xs[0].shape[axis]
    out = None
    for j, y in enumerate(ys):
        lhs = xs[:n - j]
        if not lhs:
            continue
        stacked = lhs[0] if len(lhs) == 1 else jnp.concatenate(lhs, axis=axis)
        prod = lax.dot_general(stacked, y, dims, preferred_element_type=F32)
        for i in range(len(lhs)):
            part = prod[i * rows:(i + 1) * rows]
            out = part if out is None else out + part
    return out


def _ffn_kernel(*refs, final_norm, cast_next):
    x_ref, g_ref, wg_ref, wu_ref, wd_ref, fg_ref = refs[:6]
    if cast_next:
        o_ref = refs[9]
        for src, dst in zip(refs[6:9], refs[10:13]):
            dst[...] = src[...].astype(BF16)
    else:
        o_ref = refs[6]
    h_ref, act_ref = refs[-2:]
    sub = TM_FFN // FFN_SUBTILES
    n_chunks = D_FF // FF_CHUNK
    step = sub // FFN_NORM_SLICES

    def normalize(r0, r1):
        h_ref[r0:r1, :] = _rmsnorm(x_ref[r0:r1, :], g_ref[...]).astype(BF16)

    normalize(0, sub)
    for s in range(FFN_SUBTILES):
        rows = slice(s * sub, (s + 1) * sub)
        h = h_ref[rows, :]
        for j in range(n_chunks):
            cols = slice(j * FF_CHUNK, (j + 1) * FF_CHUNK)
            gate = jnp.dot(h, wg_ref[:, cols], preferred_element_type=F32)
            up = jnp.dot(h, wu_ref[:, cols], preferred_element_type=F32)
            act_ref[rows, cols] = (gate * _sigmoid(gate) * up).astype(BF16)
            if s + 1 < FFN_SUBTILES and j < FFN_NORM_SLICES:
                normalize((s + 1) * sub + j * step, (s + 1) * sub + (j + 1) * step)
        y = x_ref[rows, :] + 0.5 * jnp.dot(act_ref[rows, :], wd_ref[...], preferred_element_type=F32)
        if final_norm:
            y = _rmsnorm(y, fg_ref[...])
        o_ref[rows, :] = y


def _ffn(x, g, weights, next_f32=None, final_g=None):
    T = x.shape[0]
    n_steps = T // TM_FFN
    fg = g if final_g is None else final_g
    in_specs = [_rows(D_MODEL, TM_FFN), _resident((1, D_MODEL)), _resident((D_MODEL, D_FF)),
                _resident((D_MODEL, D_FF)), _resident((D_FF, D_MODEL)), _resident((1, D_MODEL))]
    args = [x, g, *weights, fg]
    out_specs = [_rows(D_MODEL, TM_FFN)]
    out_shape = [jax.ShapeDtypeStruct((T, D_MODEL), F32)]
    if next_f32 is not None:
        *w_next, layer, k = next_f32
        for w in w_next:
            rows, cols = w.shape[2] // n_steps, w.shape[3]
            in_specs.append(pl.BlockSpec((None, None, rows, cols), lambda i: (layer, k, i, 0)))
            out_specs.append(pl.BlockSpec((rows, cols), lambda i: (i, 0)))
            out_shape.append(jax.ShapeDtypeStruct(w.shape[2:], BF16))
        args += w_next
    outs = pl.pallas_call(
        functools.partial(_ffn_kernel, final_norm=final_g is not None, cast_next=next_f32 is not None),
        grid=(n_steps,),
        in_specs=in_specs,
        out_specs=out_specs,
        out_shape=out_shape,
        scratch_shapes=[pltpu.VMEM((TM_FFN, D_MODEL), BF16), pltpu.VMEM((TM_FFN, D_FF), BF16)],
        compiler_params=_params(),
        name="ffn",
    )(*args)
    return outs[0], tuple(outs[1:])


def _rwkv_chunks(kt_s, rt_s, bt_s, kd_s, be_s, ke_s, v_s, cum_s, y_s, st_ref):
    L = CHUNK
    reps = GROUP_W // L
    row = lax.broadcasted_iota(jnp.int32, (L, GROUP_W), 0)
    col = lax.broadcasted_iota(jnp.int32, (L, GROUP_W), 1) & (HEAD_A - 1)
    r256 = lax.broadcasted_iota(jnp.int32, (GROUP_W, GROUP_W), 0)
    c256 = lax.broadcasted_iota(jnp.int32, (GROUP_W, GROUP_W), 1)
    same_head = (r256 // HEAD_A) == (c256 // HEAD_A)
    eye_ss = jnp.where(row == col, 1.0, 0.0)
    strict = row > col
    incl = row >= col
    level_masks = []
    s = 1
    while s < L:
        level_masks.append(((row ^ col) < 2 * s) & ((row & s) > (col & s)))
        s *= 2

    lane = lax.broadcasted_iota(jnp.int32, (L, LANES), 1)
    half_masks = [jnp.where(lane < HEAD_A, 1.0, 0.0).astype(BF16), jnp.where(lane >= HEAD_A, 1.0, 0.0).astype(BF16)]
    zero_tile = jnp.zeros((L, LANES), BF16)

    def bd(pieces):
        out = []
        for p in pieces:
            blocks = []
            for h in range(reps):
                lt = h * HEAD_A // LANES
                half = p[:, lt * LANES:(lt + 1) * LANES] * half_masks[h % 2]
                blocks.append(jnp.concatenate([half if t == lt else zero_tile for t in range(GROUP_W // LANES)],
                                              axis=1))
            out.append(jnp.concatenate(blocks, axis=0))
        return out

    def per_head(xs, y, n):
        return _mm_pieces(xs, bd(_split(y, n)))

    def fold(full):
        masked = jnp.where(same_head, full, 0.0)
        return sum(masked[h * L:(h + 1) * L] for h in range(reps))

    cgs = [(ci, gi) for ci in range(TT // L) for gi in range(N_GROUPS)]
    sl = {cg: (slice(cg[0] * L, (cg[0] + 1) * L), slice(cg[1] * GROUP_W, (cg[1] + 1) * GROUP_W)) for cg in cgs}
    kt_p, a_ab, a_ak, a_rb, a_rk, t_inv, r_hat, y0, m_ss, c_ss = ({} for _ in range(10))

    def scores():
        for cg in cgs:
            kt_p[cg] = _split(kt_s[sl[cg]], max(N_SCORE, N_APPLY))
            rt_p = _split(rt_s[sl[cg]], N_SCORE)
            lhs_p = [jnp.concatenate([a, b], axis=0) for a, b in zip(kt_p[cg][:N_SCORE], rt_p)]
            s_b = _mm_pieces(lhs_p, bd(_split(bt_s[sl[cg]], N_SCORE)), _NT)
            s_k = _mm_pieces(lhs_p, bd(_split(kd_s[sl[cg]], N_SCORE)), _NT)
            a_ab[cg] = s_b[:L]
            a_ak[cg] = jnp.where(strict, s_k[:L], 0.0)
            a_rb[cg] = jnp.where(incl, s_b[L:], 0.0)
            a_rk[cg] = jnp.where(incl, s_k[L:], 0.0)
            t_inv[cg] = eye_ss - jnp.where(level_masks[0], a_ab[cg], 0.0)

    def inverse_level(m):
        def run():
            t_p = {cg: _split(t_inv[cg], N_INV) for cg in cgs}
            left = {cg: per_head(t_p[cg], jnp.where(m, a_ab[cg], 0.0), N_INV) for cg in cgs}
            for cg in cgs:
                t_inv[cg] = t_inv[cg] - _mm_pieces(_split(left[cg], N_INV), bd(t_p[cg]))
        return run

    def apply():
        t_p = {cg: _split(t_inv[cg], N_APPLY) for cg in cgs}
        v_bd = {cg: bd(_split(v_s[sl[cg]], N_APPLY)) for cg in cgs}
        w_mat = {cg: -_mm_pieces(t_p[cg], bd(kt_p[cg][:N_APPLY])) for cg in cgs}
        akv = {cg: _mm_pieces(_split(a_ak[cg], N_APPLY), v_bd[cg]) for cg in cgs}
        u0 = {cg: -per_head(t_p[cg], akv[cg], N_APPLY) for cg in cgs}
        arb_p = {cg: _split(a_rb[cg], N_APPLY) for cg in cgs}
        for cg in cgs:
            r_hat[cg] = rt_s[sl[cg]] + per_head(arb_p[cg], w_mat[cg], N_APPLY)
            y0[cg] = per_head(arb_p[cg], u0[cg], N_APPLY) + _mm_pieces(_split(a_rk[cg], N_APPLY), v_bd[cg])
        for cg in cgs:
            ci = cg[0]
            p_end = jnp.exp(cum_s[ci * L + L - 1:ci * L + L, sl[cg][1]])
            left = jnp.concatenate([be_s[sl[cg]], ke_s[sl[cg]]], axis=0)
            right = jnp.concatenate([jnp.concatenate([w_mat[cg], u0[cg]], axis=1),
                                     jnp.concatenate([jnp.zeros_like(u0[cg]), v_s[sl[cg]]], axis=1)], axis=0)
            full = _mm_pieces(_split(left, N_TRANS), _split(right, N_TRANS), _TN)
            m_ss[cg] = fold(full[:, :GROUP_W]) + eye_ss * p_end
            c_ss[cg] = fold(full[:, GROUP_W:])

    def state():
        for cg in cgs:
            gi = cg[1]
            st_bd = bd(_split(st_ref[gi], N_STATE))
            both = _mm_pieces([jnp.concatenate([a, b], axis=0)
                               for a, b in zip(_split(r_hat[cg], N_STATE), _split(m_ss[cg], N_STATE))], st_bd)
            y_s[sl[cg]] = both[:L] + y0[cg]
            st_ref[gi] = both[L:] + c_ss[cg]

    return [scores] + [inverse_level(m) for m in level_masks[1:]] + [apply, state]


def _seg_sum(x, seg, n=N_SUM):
    return jnp.concatenate([_mm_pieces(_split(x[:, gi * GROUP_W:(gi + 1) * GROUP_W], n), [seg])
                            for gi in range(N_GROUPS)], axis=1)


def _even_kernel(*refs, has_vres):
    (xc_ref, xp_ref, gn_ref, win_ref, mu_ref, w0_ref, wup_ref, a0_ref, aup_ref, gup_ref, kk_ref, ka_ref, rk_ref,
     lnw_ref, lnb_ref, seg_ref, cw_ref, wo_ref) = refs[:18]
    n_in = 18
    if has_vres:
        vf_ref, v0_ref, vdn_ref, vup_ref = refs[18:22]
        n_in = 22
        o_ref = refs[n_in]
        scratch = refs[n_in + 1:]
    else:
        o_ref, vout_ref = refs[n_in:n_in + 2]
        scratch = refs[n_in + 2:]
    (ext_ref, extc_ref, bonus_s, g_s, yb_s, v_s, kt_s, rt_s, bt_s, kd_s, be_s, ke_s, cum_s, y_s, st_ref) = scratch
    L = CHUNK
    i = pl.program_id(0)
    n_tiles = pl.num_programs(0) - 1

    @pl.when(i == 0)
    def _():
        ext_ref[0:HALO, :] = jnp.zeros((HALO, D_IN_A), F32)
        extc_ref[0:HALO, :] = jnp.zeros((HALO, D_B), F32)
        st_ref[...] = jnp.zeros_like(st_ref)
        for ref in (bonus_s, g_s, yb_s, v_s, kt_s, rt_s, bt_s, kd_s, be_s, ke_s, cum_s):
            ref[...] = jnp.zeros_like(ref)

    late = _rwkv_chunks(kt_s, rt_s, bt_s, kd_s, be_s, ke_s, v_s, cum_s, y_s, st_ref)

    def finish():
        y = y_s[...]
        inv_n = 1.0 / HEAD_A
        mean = _seg_sum(y, seg_ref[...]) * inv_n
        d = y - mean
        var = _seg_sum(d * d, seg_ref[...], 1) * inv_n
        y = d * lax.rsqrt(var + GN_EPS) * lnw_ref[...] + lnb_ref[...]
        ya = (y + bonus_s[...]) * g_s[...]
        mix = _bdot(jnp.concatenate([ya, yb_s[...]], axis=-1), wo_ref[...])
        o_ref[...] = xp_ref[...] + mix

    e = {}

    def project(c0, c1):
        p = jnp.dot(e["h"], win_ref[:, c0:c1], preferred_element_type=F32)
        ext_ref[HALO:HALO + TT, c0:c1] = p
        prev = ext_ref[HALO - 1:HALO - 1 + TT, c0:c1]
        ext_ref[0:HALO, c0:c1] = p[TT - HALO:TT, :]
        return p + (prev - p) * mu_ref[:, c0:c1]

    def early_keys():
        k = project(D_A, 2 * D_A)
        lx = project(3 * D_A, D_IN_A)
        x_wa = lx[:, :LANES]
        z = w0_ref[...] + _bdot(jnp.tanh(x_wa), wup_ref[...])
        e["wl"] = -DECAY_SCALE * _sigmoid(z)
        a = _sigmoid(a0_ref[...] + _bdot(x_wa, aup_ref[...]))
        e["g"] = _bdot(_sigmoid(lx[:, LANES:]), gup_ref[...])
        kn = k * kk_ref[...]
        e["kn"] = kn * lax.rsqrt(jnp.maximum(_seg_sum(kn * kn, seg_ref[...], 1), 1e-24))
        e["k"] = k * (1.0 + (a - 1.0) * ka_ref[...])
        e["b"] = e["kn"] * a

    def early_decay():
        ti = lax.broadcasted_iota(jnp.int32, (TT, TT), 0)
        tj = lax.broadcasted_iota(jnp.int32, (TT, TT), 1)
        tri = jnp.where((ti // L) == (tj // L), jnp.where(tj <= ti, 1.0, 0.0), 0.0).astype(BF16)
        cum = _mm_pieces([tri], _split(e["wl"], N_SUM))
        to_end = jnp.concatenate([cum[c * L + L - 1:c * L + L, :] - cum[c * L:(c + 1) * L, :]
                                  for c in range(TT // L)], axis=0)
        inv_p = jnp.exp(-cum)
        e_end = jnp.exp(to_end)
        e["cum"] = cum
        e["be"] = e["b"] * e_end
        e["ke"] = e["k"] * e_end
        kt_s[...] = e["kn"] * jnp.exp(cum - e["wl"])
        bt_s[...] = e["b"] * inv_p
        kd_s[...] = e["k"] * inv_p

    def early_values():
        r = project(0, D_A)
        v = project(2 * D_A, 3 * D_A)
        if has_vres:
            low = _bdot(v, vdn_ref[...])
            v = v + (vf_ref[...] - v) * _sigmoid(v0_ref[...] + _bdot(low, vup_ref[...]))
        e["v"] = v
        e["rt"] = r * jnp.exp(e["cum"])
        e["bonus"] = _seg_sum(r * e["k"] * rk_ref[...], seg_ref[...], 1) * v

    def early_conv():
        pb = jnp.dot(e["h"], win_ref[:, D_IN_A:], preferred_element_type=F32)
        ch = pb[:, D_B:2 * D_B] * pb[:, 2 * D_B:]
        extc_ref[HALO:HALO + TT, :] = ch
        conv = (cw_ref[0:1, :] * extc_ref[HALO - 2:HALO - 2 + TT, :]
                + cw_ref[1:2, :] * extc_ref[HALO - 1:HALO - 1 + TT, :]
                + cw_ref[2:3, :] * ch)
        extc_ref[0:HALO, :] = ch[TT - HALO:TT, :]
        e["yb"] = pb[:, :D_B] * conv

    for stage in late:
        stage()
    finish()
    e["h"] = _rmsnorm(xc_ref[...], gn_ref[...]).astype(BF16)
    for stage in (early_keys, early_decay, early_values, early_conv):
        stage()
    rt_s[...] = e["rt"]
    v_s[...] = e["v"]
    be_s[...] = e["be"]
    ke_s[...] = e["ke"]
    cum_s[...] = e["cum"]
    bonus_s[...] = e["bonus"]
    g_s[...] = e["g"]
    yb_s[...] = e["yb"]
    if not has_vres:
        @pl.when(i < n_tiles)
        def _():
            vout_ref[...] = e["v"]


def _even_mix(x, g, prm, v_first, vres):
    T = x.shape[0]
    n_tiles = T // TT
    has_vres = vres is not None
    cur = lambda w: pl.BlockSpec((TT, w), lambda i: (jnp.minimum(i, n_tiles - 1), 0))
    prv = lambda w: pl.BlockSpec((TT, w), lambda i: (jnp.maximum(i - 1, 0), 0))
    vec = _resident((1, D_A))
    lora = _resident((LANES, D_A))
    in_specs = [cur(D_MODEL), prv(D_MODEL), _resident((1, D_MODEL)), _resident((D_MODEL, D_IN_A + 3 * D_B)),
                _resident((1, D_IN_A)), vec, lora, vec, lora, lora, vec, vec, vec, vec, vec,
                _resident((GROUP_W, GROUP_W)), _resident(prm["conv_w"].shape), _resident((D_MODEL, D_MODEL))]
    args = [x, x, g, prm["w_in"], prm["mu"], prm["w0"], prm["w_up"], prm["a0"], prm["a_up"], prm["g_up"], prm["k_k"],
            prm["k_a"], prm["r_k"], prm["ln_w"], prm["ln_b"], prm["seg"], prm["conv_w"], prm["w_out"]]
    out_specs = [prv(D_MODEL)]
    out_shape = [jax.ShapeDtypeStruct((T, D_MODEL), F32)]
    if has_vres:
        in_specs += [cur(D_A), vec, _resident((D_A, LANES)), lora]
        args += [v_first, vres["v0"], vres["v_dn"], vres["v_up"]]
    else:
        out_specs.append(cur(D_A))
        out_shape.append(jax.ShapeDtypeStruct((T, D_A), F32))
    tile = pltpu.VMEM((TT, D_A), F32)
    scratch = ([pltpu.VMEM((HALO + TT, D_IN_A), F32), pltpu.VMEM((HALO + TT, D_B), F32)] + [tile] * 12
               + [pltpu.VMEM((N_GROUPS, CHUNK, GROUP_W), F32)])
    outs = pl.pallas_call(
        functools.partial(_even_kernel, has_vres=has_vres),
        grid=(n_tiles + 1,),
        in_specs=in_specs,
        out_specs=out_specs,
        out_shape=out_shape,
        scratch_shapes=scratch,
        compiler_params=_params(),
        name="even_mix",
    )(*args)
    return (outs[0], v_first) if has_vres else (outs[0], outs[1])


def _gelu_tanh(x):
    return x * (0.5 * (1.0 + jnp.tanh(0.7978845608028654 * (x + 0.044715 * (x * x * x)))))


def _linear_scan(a_ref, b_ref, h_ref, h0):
    n_blocks = a_ref.shape[0]
    seg = SCAN_SEG
    rows = lambda j: pl.ds(j, SUBLANES, stride=SCAN_PITCH)
    blocks = range(n_blocks)
    prods = [[a_ref[c, rows(0), :]] for c in blocks]
    accs = [[b_ref[c, rows(0), :]] for c in blocks]
    for j in range(1, seg):
        for c in blocks:
            aj = a_ref[c, rows(j), :]
            accs[c].append(aj * accs[c][-1] + b_ref[c, rows(j), :])
            prods[c].append(aj * prods[c][-1])
    starts = [[h0[:, c * LANES:(c + 1) * LANES]] for c in blocks]
    for r in range(SUBLANES - 1):
        for c in blocks:
            starts[c].append(prods[c][-1][r:r + 1, :] * starts[c][r] + accs[c][-1][r:r + 1, :])
    start = [jnp.concatenate(starts[c], axis=0) for c in blocks]
    for j in range(seg):
        for c in blocks:
            h_ref[c, rows(j), :] = accs[c][j] + prods[c][j] * start[c]


def _odd_kernel(xc_ref, xp_ref, g_ref, win_ref, cw_ref, cb_ref, wa_ref, ba_ref, wx_ref, bx_ref, lam_ref, dw_ref,
                ds_ref, wo_ref, o_ref, pc_s, pd_s, gact_s, sa_s, sb_s, hs_s, extu_ref, extd_ref, h_ref):
    i = pl.program_id(0)

    @pl.when(i == 0)
    def _():
        extu_ref[0:HALO, :] = jnp.zeros((HALO, D_C), F32)
        extd_ref[0:POOL_HALO, :] = jnp.zeros((POOL_HALO, D_D), F32)
        h_ref[...] = jnp.zeros_like(h_ref)
        pc_s[...] = jnp.zeros_like(pc_s)
        pd_s[...] = jnp.zeros_like(pd_s)

    gact_s[...] = _gelu_tanh(pc_s[:, :D_C])
    extu_ref[HALO:HALO + TT, :] = pc_s[:, D_C:]
    extd_ref[POOL_HALO:POOL_HALO + TT, :] = pd_s[...]

    h = _rmsnorm(xc_ref[...], g_ref[...]).astype(BF16)

    def project(blocks):
        for n in blocks:
            cols = slice(n * MXU_DIM, (n + 1) * MXU_DIM)
            p = jnp.dot(h, win_ref[:, cols], preferred_element_type=F32)
            if n < 2 * D_C // MXU_DIM:
                pc_s[:, cols] = p
            else:
                pd_s[:, n * MXU_DIM - 2 * D_C:(n + 1) * MXU_DIM - 2 * D_C] = p

    t_glob = (i - 1) * TT + lax.broadcasted_iota(jnp.int32, (TT, LANES), 0)

    u_in = extu_ref[HALO:HALO + TT, :]
    u = cb_ref[...] + cw_ref[3:4, :] * u_in
    for j in range(3):
        u = u + cw_ref[j:j + 1, :] * extu_ref[HALO - 3 + j:HALO - 3 + j + TT, :]
    extu_ref[0:HALO, :] = u_in[TT - HALO:TT, :]
    project((0, 1))
    rec = _sigmoid(_bdot(u, wa_ref[...]) + ba_ref[...])
    inp = _sigmoid(_bdot(u, wx_ref[...]) + bx_ref[...])
    log_a = (-LRU_C) * rec * _softplus(-lam_ref[...])
    a = jnp.exp(log_a)
    mult = jnp.sqrt(-jnp.tanh(log_a) * (a * a + 1.0))
    row = lax.broadcasted_iota(jnp.int32, (TT, D_C), 0)
    mult = jnp.where(row + (i - 1) * TT == 0, 1.0, mult)
    project((2, 3))
    b = mult * inp * u
    for c in range(D_C // LANES):
        for r in range(SUBLANES):
            sa_s[c, r * SCAN_PITCH:r * SCAN_PITCH + SCAN_SEG, :] = a[r * SCAN_SEG:(r + 1) * SCAN_SEG,
                                                                     c * LANES:(c + 1) * LANES]
            sb_s[c, r * SCAN_PITCH:r * SCAN_PITCH + SCAN_SEG, :] = b[r * SCAN_SEG:(r + 1) * SCAN_SEG,
                                                                     c * LANES:(c + 1) * LANES]
    _linear_scan(sa_s, sb_s, hs_s, h_ref[...])
    hs = jnp.concatenate(
        [jnp.concatenate([hs_s[c, r * SCAN_PITCH:r * SCAN_PITCH + SCAN_SEG, :] for r in range(SUBLANES)], axis=0)
         for c in range(D_C // LANES)], axis=1)
    h_ref[...] = jnp.where(i > 0, hs[TT - 1:TT, :], 0.0)
    yc = gact_s[...] * hs
    project((4, 5))

    parts = []
    for gi, win in enumerate(POOL_WINDOWS):
        e = extd_ref[:, gi * G_D:(gi + 1) * G_D]
        span = 1
        while span < win:
            e = e + pltpu.roll(e, span, 0)
            span *= 2
        n_avail = jnp.clip(t_glob + 1, 1, win).astype(F32)
        parts.append(e[POOL_HALO:, :] / n_avail - extd_ref[POOL_HALO:POOL_HALO + TT, gi * G_D:(gi + 1) * G_D])
    extd_ref[0:POOL_HALO, :] = extd_ref[TT:TT + POOL_HALO, :]
    yd = _bdot(jnp.concatenate(parts, axis=-1), dw_ref[...]) * ds_ref[...]

    mix = _bdot(jnp.concatenate([yc, yd], axis=-1), wo_ref[...])
    o_ref[...] = xp_ref[...] + mix


def _odd_mix(x, g, prm):
    T = x.shape[0]
    n_tiles = T // TT
    cur = pl.BlockSpec((TT, D_MODEL), lambda i: (jnp.minimum(i, n_tiles - 1), 0))
    prv = pl.BlockSpec((TT, D_MODEL), lambda i: (jnp.maximum(i - 1, 0), 0))
    vec = _resident((1, D_C))
    sq = _resident((D_C, D_C))
    return pl.pallas_call(
        _odd_kernel,
        grid=(n_tiles + 1,),
        in_specs=[cur, prv, _resident((1, D_MODEL)), _resident((D_MODEL, 2 * D_C + D_D)),
                  _resident(prm["conv_w"].shape), vec, sq, vec, sq, vec, vec, sq, vec,
                  _resident((D_MODEL, D_MODEL))],
        out_specs=prv,
        out_shape=jax.ShapeDtypeStruct((T, D_MODEL), F32),
        scratch_shapes=([pltpu.VMEM((TT, 2 * D_C), F32), pltpu.VMEM((TT, D_D), F32)]
                        + [pltpu.VMEM((TT, D_C), F32)] + [pltpu.VMEM((D_C // LANES, SUBLANES * SCAN_PITCH, LANES), F32)] * 3
                        + [pltpu.VMEM((HALO + TT, D_C), F32), pltpu.VMEM((POOL_HALO + TT, D_D), F32),
                           pltpu.VMEM((1, D_C), F32)]),
        compiler_params=_params(),
        name="odd_mix",
    )(x, x, g, prm["w_in"], prm["conv_w"], prm["conv_b"], prm["wa"], prm["ba"], prm["wx"], prm["bx"], prm["lam"],
      prm["dw"], prm["ds"], prm["w_out"])


def _block_diag(w):
    H, n, m = w.shape
    return jnp.einsum("hij,hg->higj", w, jnp.eye(H, dtype=w.dtype)).reshape(H * n, H * m)


def _pad_rows(w, before, total):
    return jnp.pad(w, ((before, total - before - w.shape[0]), (0, 0)))


def kernel(x, norm_g, ffn_wg, ffn_wu, ffn_wd, even_w_in, even_w_out, a_mu, a_w0, a_w_up, a_a0, a_a_up, a_g_up,
           a_k_k, a_k_a, a_r_k, a_ln_w, a_ln_b, a_v0, a_v_dn, a_v_up, b_conv_w, odd_w_in, odd_w_out, c_conv_w,
           c_conv_b, c_wa, c_ba, c_wx, c_bx, c_lam, d_w, d_scale, final_g):
    B, T, D = x.shape
    assert (B, D) == (1, D_MODEL) and T % TM_FFN == 0 and T % TT == 0
    xt = x.reshape(T, D)
    ffn_f32 = (ffn_wg, ffn_wu, ffn_wd)
    w_ffn = tuple(w[0, 0].astype(BF16) for w in ffn_f32)
    n_ffn = 2 * DEPTH
    following = lambda j: (*ffn_f32, (j + 1) // 2, (j + 1) % 2) if j + 1 < n_ffn else None
    row = lambda v: v.reshape(1, -1)
    lora_w = a_w_up.shape[1]
    seg = _block_diag(jnp.ones((GROUP_W // HEAD_A, HEAD_A, HEAD_A), BF16))
    v_first = None
    for layer in range(DEPTH):
        xt, w_ffn = _ffn(xt, row(norm_g[layer, 0]), w_ffn, following(2 * layer))
        g_mix = row(norm_g[layer, 1])
        if layer % 2 == 0:
            e = layer // 2
            prm = dict(
                w_in=even_w_in[e].astype(BF16), w_out=even_w_out[e].astype(BF16), conv_w=b_conv_w[e],
                mu=row(a_mu[e]), w0=row(a_w0[e]), a0=row(a_a0[e]),
                w_up=_pad_rows(a_w_up[e], 0, LANES).astype(BF16),
                a_up=_pad_rows(a_a_up[e], lora_w, LANES).astype(BF16),
                g_up=a_g_up[e].astype(BF16),
                k_k=row(a_k_k[e]), k_a=row(a_k_a[e]), r_k=row(a_r_k[e]), ln_w=row(a_ln_w[e]), ln_b=row(a_ln_b[e]),
                seg=seg)
            vres = None
            if e > 0:
                vres = dict(v0=row(a_v0[e - 1]),
                            v_dn=jnp.pad(a_v_dn[e - 1], ((0, 0), (0, LANES - LORA_V))).astype(BF16),
                            v_up=_pad_rows(a_v_up[e - 1], 0, LANES).astype(BF16))
            xt, v_first = _even_mix(xt, g_mix, prm, v_first, vres)
        else:
            o = layer // 2
            prm = dict(
                w_in=odd_w_in[o].astype(BF16), w_out=odd_w_out[o].astype(BF16),
                conv_w=c_conv_w[o], conv_b=row(c_conv_b[o]),
                wa=_block_diag(c_wa[o]).astype(BF16), ba=row(c_ba[o]),
                wx=_block_diag(c_wx[o]).astype(BF16), bx=row(c_bx[o]), lam=row(c_lam[o]),
                dw=_block_diag(d_w[o]).astype(BF16), ds=row(d_scale[o]))
            xt = _odd_mix(xt, g_mix, prm)
        last = layer == DEPTH - 1
        xt, w_ffn = _ffn(xt, row(norm_g[layer, 2]), w_ffn, following(2 * layer + 1),
                         final_g=row(final_g) if last else None)
    return xt.reshape(B, T, D)
```

```python
import functools

import jax
import jax.numpy as jnp
from jax import lax
from jax.experimental import pallas as pl
from jax.experimental.pallas import tpu as pltpu

F32 = jnp.float32
BF16 = jnp.bfloat16

D_MODEL = 1024
DEPTH = 4
D_A = 512
HEAD_A = 64
LORA_V = 32
D_IN_A = 1792
D_B = 512
D_C = 512
H_C = 8
D_D = 512
POOL_WINDOWS = (2, 4, 8, 16)
G_D = 128
D_FF = 2816
GN_EPS = 64e-5
RMS_EPS = 1e-6
LRU_C = 8.0
DECAY_SCALE = 0.6065306597126334

LANES = 128
SUBLANES = 8
MXU_DIM = 256
VMEM_LIMIT_BYTES = 56 * 1024 * 1024

TM_FFN = 1024
FF_CHUNK = MXU_DIM
FFN_SUBTILES = 4
FFN_NORM_SLICES = 8
TT = 256
TT_ODD = 512
CHUNK = 64
GROUP_W = MXU_DIM
N_GROUPS = D_A // GROUP_W
HALO = SUBLANES
POOL_HALO = 2 * SUBLANES

N_SCORE = 1
N_INV = 1
N_APPLY = 1
N_TRANS = 1
N_STATE = 2
N_SUM = 2

_NN = (((1,), (0,)), ((), ()))
_NT = (((1,), (1,)), ((), ()))
_TN = (((0,), (0,)), ((), ()))


def _params(n_axes=1):
    return pltpu.CompilerParams(dimension_semantics=("arbitrary",) * n_axes,
                                vmem_limit_bytes=VMEM_LIMIT_BYTES)


def _resident(shape):
    zeros = (0,) * len(shape)
    return pl.BlockSpec(shape, lambda i: zeros, pipeline_mode=pl.Buffered(1))


def _rows(width, tile):
    return pl.BlockSpec((tile, width), lambda i: (i, 0))


def _rmsnorm(x, g):
    ms = jnp.mean(x * x, axis=-1, keepdims=True)
    return x * lax.rsqrt(ms + RMS_EPS) * g


def _sigmoid(x):
    return 1.0 / (1.0 + jnp.exp(-x))


def _softplus(z):
    return jnp.maximum(z, 0.0) + jnp.log1p(jnp.exp(-jnp.abs(z)))


def _bdot(x, w):
    return jnp.dot(x.astype(BF16), w, preferred_element_type=F32)


def _split(x, n):
    pieces = []
    for _ in range(n - 1):
        p = x.astype(BF16)
        pieces.append(p)
        x = x - p.astype(F32)
    pieces.append(x.astype(BF16))
    return pieces


def _mm_pieces(xs, ys, dims=_NN):
    n = max(len(xs), len(ys))
    axis = 1 if dims == _TN else 0
    rows = xs[0].shape[axis]
    out = None
    for j, y in enumerate(ys):
        lhs = xs[:n - j]
        if not lhs:
            continue
        stacked = lhs[0] if len(lhs) == 1 else jnp.concatenate(lhs, axis=axis)
        prod = lax.dot_general(stacked, y, dims, preferred_element_type=F32)
        for i in range(len(lhs)):
            part = prod[i * rows:(i + 1) * rows]
            out = part if out is None else out + part
    return out


def _ffn_kernel(*refs, final_norm, cast_next):
    x_ref, g_ref, wg_ref, wu_ref, wd_ref, fg_ref = refs[:6]
    if cast_next:
        o_ref = refs[9]
        for src, dst in zip(refs[6:9], refs[10:13]):
            dst[...] = src[...].astype(BF16)
    else:
        o_ref = refs[6]
    h_ref, act_ref = refs[-2:]
    sub = TM_FFN // FFN_SUBTILES
    n_chunks = D_FF // FF_CHUNK
    step = sub // FFN_NORM_SLICES

    def normalize(r0, r1):
        h_ref[r0:r1, :] = _rmsnorm(x_ref[r0:r1, :], g_ref[...]).astype(BF16)

    normalize(0, sub)
    for s in range(FFN_SUBTILES):
        rows = slice(s * sub, (s + 1) * sub)
        h = h_ref[rows, :]
        for j in range(n_chunks):
            cols = slice(j * FF_CHUNK, (j + 1) * FF_CHUNK)
            gate = jnp.dot(h, wg_ref[:, cols], preferred_element_type=F32)
            up = jnp.dot(h, wu_ref[:, cols], preferred_element_type=F32)
            act_ref[rows, cols] = (gate * _sigmoid(gate) * up).astype(BF16)
            if s + 1 < FFN_SUBTILES and j < FFN_NORM_SLICES:
                normalize((s + 1) * sub + j * step, (s + 1) * sub + (j + 1) * step)
        y = x_ref[rows, :] + 0.5 * jnp.dot(act_ref[rows, :], wd_ref[...], preferred_element_type=F32)
        if final_norm:
            y = _rmsnorm(y, fg_ref[...])
        o_ref[rows, :] = y


def _ffn(x, g, weights, next_f32=None, final_g=None):
    T = x.shape[0]
    n_steps = T // TM_FFN
    fg = g if final_g is None else final_g
    in_specs = [_rows(D_MODEL, TM_FFN), _resident((1, D_MODEL)), _resident((D_MODEL, D_FF)),
                _resident((D_MODEL, D_FF)), _resident((D_FF, D_MODEL)), _resident((1, D_MODEL))]
    args = [x, g, *weights, fg]
    out_specs = [_rows(D_MODEL, TM_FFN)]
    out_shape = [jax.ShapeDtypeStruct((T, D_MODEL), F32)]
    if next_f32 is not None:
        *w_next, layer, k = next_f32
        for w in w_next:
            rows, cols = w.shape[2] // n_steps, w.shape[3]
            in_specs.append(pl.BlockSpec((None, None, rows, cols), lambda i: (layer, k, i, 0)))
            out_specs.append(pl.BlockSpec((rows, cols), lambda i: (i, 0)))
            out_shape.append(jax.ShapeDtypeStruct(w.shape[2:], BF16))
        args += w_next
    outs = pl.pallas_call(
        functools.partial(_ffn_kernel, final_norm=final_g is not None, cast_next=next_f32 is not None),
        grid=(n_steps,),
        in_specs=in_specs,
        out_specs=out_specs,
        out_shape=out_shape,
        scratch_shapes=[pltpu.VMEM((TM_FFN, D_MODEL), BF16), pltpu.VMEM((TM_FFN, D_FF), BF16)],
        compiler_params=_params(),
        name="ffn",
    )(*args)
    return outs[0], tuple(outs[1:])


def _rwkv_chunks(kt_s, rt_s, bt_s, kd_s, be_s, ke_s, v_s, cum_s, y_s, st_ref):
    L = CHUNK
    reps = GROUP_W // L
    row = lax.broadcasted_iota(jnp.int32, (L, GROUP_W), 0)
    col = lax.broadcasted_iota(jnp.int32, (L, GROUP_W), 1) & (HEAD_A - 1)
    r256 = lax.broadcasted_iota(jnp.int32, (GROUP_W, GROUP_W), 0)
    c256 = lax.broadcasted_iota(jnp.int32, (GROUP_W, GROUP_W), 1)
    same_head = (r256 // HEAD_A) == (c256 // HEAD_A)
    eye_ss = jnp.where(row == col, 1.0, 0.0)
    strict = row > col
    incl = row >= col
    level_masks = []
    s = 1
    while s < L:
        level_masks.append(((row ^ col) < 2 * s) & ((row & s) > (col & s)))
        s *= 2

    lane = lax.broadcasted_iota(jnp.int32, (L, LANES), 1)
    half_masks = [jnp.where(lane < HEAD_A, 1.0, 0.0).astype(BF16), jnp.where(lane >= HEAD_A, 1.0, 0.0).astype(BF16)]
    zero_tile = jnp.zeros((L, LANES), BF16)

    def bd(pieces):
        out = []
        for p in pieces:
            blocks = []
            for h in range(reps):
                lt = h * HEAD_A // LANES
                half = p[:, lt * LANES:(lt + 1) * LANES] * half_masks[h % 2]
                blocks.append(jnp.concatenate([half if t == lt else zero_tile for t in range(GROUP_W // LANES)],
                                              axis=1))
            out.append(jnp.concatenate(blocks, axis=0))
        return out

    def per_head(xs, y, n):
        return _mm_pieces(xs, bd(_split(y, n)))

    def fold(full):
        masked = jnp.where(same_head, full, 0.0)
        return sum(masked[h * L:(h + 1) * L] for h in range(reps))

    cgs = [(ci, gi) for ci in range(TT // L) for gi in range(N_GROUPS)]
    sl = {cg: (slice(cg[0] * L, (cg[0] + 1) * L), slice(cg[1] * GROUP_W, (cg[1] + 1) * GROUP_W)) for cg in cgs}
    kt_p, a_ab, a_ak, a_rb, a_rk, t_inv, r_hat, y0, m_ss, c_ss = ({} for _ in range(10))

    def scores():
        for cg in cgs:
            kt_p[cg] = _split(kt_s[sl[cg]], max(N_SCORE, N_APPLY))
            rt_p = _split(rt_s[sl[cg]], N_SCORE)
            lhs_p = [jnp.concatenate([a, b], axis=0) for a, b in zip(kt_p[cg][:N_SCORE], rt_p)]
            s_b = _mm_pieces(lhs_p, bd(_split(bt_s[sl[cg]], N_SCORE)), _NT)
            s_k = _mm_pieces(lhs_p, bd(_split(kd_s[sl[cg]], N_SCORE)), _NT)
            a_ab[cg] = s_b[:L]
            a_ak[cg] = jnp.where(strict, s_k[:L], 0.0)
            a_rb[cg] = jnp.where(incl, s_b[L:], 0.0)
            a_rk[cg] = jnp.where(incl, s_k[L:], 0.0)
            t_inv[cg] = eye_ss - jnp.where(level_masks[0], a_ab[cg], 0.0)

    def inverse_level(m):
        def run():
            t_p = {cg: _split(t_inv[cg], N_INV) for cg in cgs}
            left = {cg: per_head(t_p[cg], jnp.where(m, a_ab[cg], 0.0), N_INV) for cg in cgs}
            for cg in cgs:
                t_inv[cg] = t_inv[cg] - _mm_pieces(_split(left[cg], N_INV), bd(t_p[cg]))
        return run

    def apply():
        t_p = {cg: _split(t_inv[cg], N_APPLY) for cg in cgs}
        v_bd = {cg: bd(_split(v_s[sl[cg]], N_APPLY)) for cg in cgs}
        w_mat = {cg: -_mm_pieces(t_p[cg], bd(kt_p[cg][:N_APPLY])) for cg in cgs}
        akv = {cg: _mm_pieces(_split(a_ak[cg], N_APPLY), v_bd[cg]) for cg in cgs}
        u0 = {cg: -per_head(t_p[cg], akv[cg], N_APPLY) for cg in cgs}
        arb_p = {cg: _split(a_rb[cg], N_APPLY) for cg in cgs}
        for cg in cgs:
            r_hat[cg] = rt_s[sl[cg]] + per_head(arb_p[cg], w_mat[cg], N_APPLY)
            y0[cg] = per_head(arb_p[cg], u0[cg], N_APPLY) + _mm_pieces(_split(a_rk[cg], N_APPLY), v_bd[cg])
        for cg in cgs:
            ci = cg[0]
            p_end = jnp.exp(cum_s[ci * L + L - 1:ci * L + L, sl[cg][1]])
            left = jnp.concatenate([be_s[sl[cg]], ke_s[sl[cg]]], axis=0)
            right = jnp.concatenate([jnp.concatenate([w_mat[cg], u0[cg]], axis=1),
                                     jnp.concatenate([jnp.zeros_like(u0[cg]), v_s[sl[cg]]], axis=1)], axis=0)
            full = _mm_pieces(_split(left, N_TRANS), _split(right, N_TRANS), _TN)
            m_ss[cg] = fold(full[:, :GROUP_W]) + eye_ss * p_end
            c_ss[cg] = fold(full[:, GROUP_W:])

    def state():
        for cg in cgs:
            gi = cg[1]
            st_bd = bd(_split(st_ref[gi], N_STATE))
            both = _mm_pieces([jnp.concatenate([a, b], axis=0)
                               for a, b in zip(_split(r_hat[cg], N_STATE), _split(m_ss[cg], N_STATE))], st_bd)
            y_s[sl[cg]] = both[:L] + y0[cg]
            st_ref[gi] = both[L:] + c_ss[cg]

    return [scores] + [inverse_level(m) for m in level_masks[1:]] + [apply, state]


def _seg_sum(x, seg, n=N_SUM):
    return jnp.concatenate([_mm_pieces(_split(x[:, gi * GROUP_W:(gi + 1) * GROUP_W], n), [seg])
                            for gi in range(N_GROUPS)], axis=1)


def _even_kernel(*refs, has_vres):
    (xc_ref, xp_ref, gn_ref, win_ref, mu_ref, w0_ref, wup_ref, a0_ref, aup_ref, gup_ref, kk_ref, ka_ref, rk_ref,
     lnw_ref, lnb_ref, seg_ref, cw_ref, wo_ref) = refs[:18]
    n_in = 18
    if has_vres:
        vf_ref, v0_ref, vdn_ref, vup_ref = refs[18:22]
        n_in = 22
        o_ref = refs[n_in]
        scratch = refs[n_in + 1:]
    else:
        o_ref, vout_ref = refs[n_in:n_in + 2]
        scratch = refs[n_in + 2:]
    (ext_ref, extc_ref, bonus_s, g_s, yb_s, v_s, kt_s, rt_s, bt_s, kd_s, be_s, ke_s, cum_s, y_s, st_ref) = scratch
    L = CHUNK
    i = pl.program_id(0)
    n_tiles = pl.num_programs(0) - 1

    @pl.when(i == 0)
    def _():
        ext_ref[0:HALO, :] = jnp.zeros((HALO, D_IN_A), F32)
        extc_ref[0:HALO, :] = jnp.zeros((HALO, D_B), F32)
        st_ref[...] = jnp.zeros_like(st_ref)
        for ref in (bonus_s, g_s, yb_s, v_s, kt_s, rt_s, bt_s, kd_s, be_s, ke_s, cum_s):
            ref[...] = jnp.zeros_like(ref)

    late = _rwkv_chunks(kt_s, rt_s, bt_s, kd_s, be_s, ke_s, v_s, cum_s, y_s, st_ref)

    def finish():
        y = y_s[...]
        inv_n = 1.0 / HEAD_A
        mean = _seg_sum(y, seg_ref[...]) * inv_n
        d = y - mean
        var = _seg_sum(d * d, seg_ref[...], 1) * inv_n
        y = d * lax.rsqrt(var + GN_EPS) * lnw_ref[...] + lnb_ref[...]
        ya = (y + bonus_s[...]) * g_s[...]
        mix = _bdot(jnp.concatenate([ya, yb_s[...]], axis=-1), wo_ref[...])
        o_ref[...] = xp_ref[...] + mix

    e = {}

    def project(c0, c1):
        p = jnp.dot(e["h"], win_ref[:, c0:c1], preferred_element_type=F32)
        ext_ref[HALO:HALO + TT, c0:c1] = p
        prev = ext_ref[HALO - 1:HALO - 1 + TT, c0:c1]
        ext_ref[0:HALO, c0:c1] = p[TT - HALO:TT, :]
        return p + (prev - p) * mu_ref[:, c0:c1]

    def early_keys():
        k = project(D_A, 2 * D_A)
        lx = project(3 * D_A, D_IN_A)
        x_wa = lx[:, :LANES]
        z = w0_ref[...] + _bdot(jnp.tanh(x_wa), wup_ref[...])
        e["wl"] = -DECAY_SCALE * _sigmoid(z)
        a = _sigmoid(a0_ref[...] + _bdot(x_wa, aup_ref[...]))
        e["g"] = _bdot(_sigmoid(lx[:, LANES:]), gup_ref[...])
        kn = k * kk_ref[...]
        e["kn"] = kn * lax.rsqrt(jnp.maximum(_seg_sum(kn * kn, seg_ref[...], 1), 1e-24))
        e["k"] = k * (1.0 + (a - 1.0) * ka_ref[...])
        e["b"] = e["kn"] * a

    def early_decay():
        ti = lax.broadcasted_iota(jnp.int32, (TT, TT), 0)
        tj = lax.broadcasted_iota(jnp.int32, (TT, TT), 1)
        tri = jnp.where((ti // L) == (tj // L), jnp.where(tj <= ti, 1.0, 0.0), 0.0).astype(BF16)
        cum = _mm_pieces([tri], _split(e["wl"], N_SUM))
        to_end = jnp.concatenate([cum[c * L + L - 1:c * L + L, :] - cum[c * L:(c + 1) * L, :]
                                  for c in range(TT // L)], axis=0)
        inv_p = jnp.exp(-cum)
        e_end = jnp.exp(to_end)
        e["cum"] = cum
        e["be"] = e["b"] * e_end
        e["ke"] = e["k"] * e_end
        kt_s[...] = e["kn"] * jnp.exp(cum - e["wl"])
        bt_s[...] = e["b"] * inv_p
        kd_s[...] = e["k"] * inv_p

    def early_values():
        r = project(0, D_A)
        v = project(2 * D_A, 3 * D_A)
        if has_vres:
            low = _bdot(v, vdn_ref[...])
            v = v + (vf_ref[...] - v) * _sigmoid(v0_ref[...] + _bdot(low, vup_ref[...]))
        e["v"] = v
        e["rt"] = r * jnp.exp(e["cum"])
        e["bonus"] = _seg_sum(r * e["k"] * rk_ref[...], seg_ref[...], 1) * v

    def early_conv():
        pb = jnp.dot(e["h"], win_ref[:, D_IN_A:], preferred_element_type=F32)
        ch = pb[:, D_B:2 * D_B] * pb[:, 2 * D_B:]
        extc_ref[HALO:HALO + TT, :] = ch
        conv = (cw_ref[0:1, :] * extc_ref[HALO - 2:HALO - 2 + TT, :]
                + cw_ref[1:2, :] * extc_ref[HALO - 1:HALO - 1 + TT, :]
                + cw_ref[2:3, :] * ch)
        extc_ref[0:HALO, :] = ch[TT - HALO:TT, :]
        e["yb"] = pb[:, :D_B] * conv

    for stage in late:
        stage()
    finish()
    e["h"] = _rmsnorm(xc_ref[...], gn_ref[...]).astype(BF16)
    for stage in (early_keys, early_decay, early_values, early_conv):
        stage()
    rt_s[...] = e["rt"]
    v_s[...] = e["v"]
    be_s[...] = e["be"]
    ke_s[...] = e["ke"]
    cum_s[...] = e["cum"]
    bonus_s[...] = e["bonus"]
    g_s[...] = e["g"]
    yb_s[...] = e["yb"]
    if not has_vres:
        @pl.when(i < n_tiles)
        def _():
            vout_ref[...] = e["v"]


def _even_mix(x, g, prm, v_first, vres):
    T = x.shape[0]
    n_tiles = T // TT
    has_vres = vres is not None
    cur = lambda w: pl.BlockSpec((TT, w), lambda i: (jnp.minimum(i, n_tiles - 1), 0))
    prv = lambda w: pl.BlockSpec((TT, w), lambda i: (jnp.maximum(i - 1, 0), 0))
    vec = _resident((1, D_A))
    lora = _resident((LANES, D_A))
    in_specs = [cur(D_MODEL), prv(D_MODEL), _resident((1, D_MODEL)), _resident((D_MODEL, D_IN_A + 3 * D_B)),
                _resident((1, D_IN_A)), vec, lora, vec, lora, lora, vec, vec, vec, vec, vec,
                _resident((GROUP_W, GROUP_W)), _resident(prm["conv_w"].shape), _resident((D_MODEL, D_MODEL))]
    args = [x, x, g, prm["w_in"], prm["mu"], prm["w0"], prm["w_up"], prm["a0"], prm["a_up"], prm["g_up"], prm["k_k"],
            prm["k_a"], prm["r_k"], prm["ln_w"], prm["ln_b"], prm["seg"], prm["conv_w"], prm["w_out"]]
    out_specs = [prv(D_MODEL)]
    out_shape = [jax.ShapeDtypeStruct((T, D_MODEL), F32)]
    if has_vres:
        in_specs += [cur(D_A), vec, _resident((D_A, LANES)), lora]
        args += [v_first, vres["v0"], vres["v_dn"], vres["v_up"]]
    else:
        out_specs.append(cur(D_A))
        out_shape.append(jax.ShapeDtypeStruct((T, D_A), F32))
    tile = pltpu.VMEM((TT, D_A), F32)
    scratch = ([pltpu.VMEM((HALO + TT, D_IN_A), F32), pltpu.VMEM((HALO + TT, D_B), F32)] + [tile] * 12
               + [pltpu.VMEM((N_GROUPS, CHUNK, GROUP_W), F32)])
    outs = pl.pallas_call(
        functools.partial(_even_kernel, has_vres=has_vres),
        grid=(n_tiles + 1,),
        in_specs=in_specs,
        out_specs=out_specs,
        out_shape=out_shape,
        scratch_shapes=scratch,
        compiler_params=_params(),
        name="even_mix",
    )(*args)
    return (outs[0], v_first) if has_vres else (outs[0], outs[1])


def _gelu_tanh(x):
    return x * (0.5 * (1.0 + jnp.tanh(0.7978845608028654 * (x + 0.044715 * (x * x * x)))))


def _linear_scan(a, b, h0):
    n, c = a.shape
    nb = n // SUBLANES
    a = a.reshape(nb, SUBLANES, c)
    b = b.reshape(nb, SUBLANES, c)
    sub = lax.broadcasted_iota(jnp.int32, (nb, SUBLANES, c), 1)
    step = 1
    while step < SUBLANES:
        valid = sub >= step
        a_prev = jnp.where(valid, pltpu.roll(a, step, 1), 1.0)
        b_prev = jnp.where(valid, pltpu.roll(b, step, 1), 0.0)
        b = a * b_prev + b
        a = a * a_prev
        step *= 2
    carry = h0
    blocks = []
    for j in range(nb):
        hj = a[j] * carry + b[j]
        blocks.append(hj)
        carry = hj[SUBLANES - 1:SUBLANES, :]
    return jnp.concatenate(blocks, axis=0)


def _odd_kernel(xc_ref, xp_ref, g_ref, win_ref, cw_ref, cb_ref, wa_ref, ba_ref, wx_ref, bx_ref, lam_ref, dw_ref,
                ds_ref, wo_ref, o_ref, pc_s, pd_s, gact_s, extu_ref, extd_ref, h_ref):
    TT = TT_ODD
    i = pl.program_id(0)

    @pl.when(i == 0)
    def _():
        extu_ref[0:HALO, :] = jnp.zeros((HALO, D_C), F32)
        extd_ref[0:POOL_HALO, :] = jnp.zeros((POOL_HALO, D_D), F32)
        h_ref[...] = jnp.zeros_like(h_ref)
        pc_s[...] = jnp.zeros_like(pc_s)
        pd_s[...] = jnp.zeros_like(pd_s)

    gact_s[...] = _gelu_tanh(pc_s[:, :D_C])
    extu_ref[HALO:HALO + TT, :] = pc_s[:, D_C:]
    extd_ref[POOL_HALO:POOL_HALO + TT, :] = pd_s[...]

    h = _rmsnorm(xc_ref[...], g_ref[...]).astype(BF16)

    def project(blocks):
        for n in blocks:
            cols = slice(n * MXU_DIM, (n + 1) * MXU_DIM)
            p = jnp.dot(h, win_ref[:, cols], preferred_element_type=F32)
            if n < 2 * D_C // MXU_DIM:
                pc_s[:, cols] = p
            else:
                pd_s[:, n * MXU_DIM - 2 * D_C:(n + 1) * MXU_DIM - 2 * D_C] = p

    t_glob = (i - 1) * TT + lax.broadcasted_iota(jnp.int32, (TT, LANES), 0)

    u_in = extu_ref[HALO:HALO + TT, :]
    u = cb_ref[...] + cw_ref[3:4, :] * u_in
    for j in range(3):
        u = u + cw_ref[j:j + 1, :] * extu_ref[HALO - 3 + j:HALO - 3 + j + TT, :]
    extu_ref[0:HALO, :] = u_in[TT - HALO:TT, :]
    project((0, 1))
    rec = _sigmoid(_bdot(u, wa_ref[...]) + ba_ref[...])
    inp = _sigmoid(_bdot(u, wx_ref[...]) + bx_ref[...])
    log_a = (-LRU_C) * rec * _softplus(-lam_ref[...])
    a = jnp.exp(log_a)
    mult = jnp.sqrt(-jnp.tanh(log_a) * (a * a + 1.0))
    row = lax.broadcasted_iota(jnp.int32, (TT, D_C), 0)
    mult = jnp.where(row + (i - 1) * TT == 0, 1.0, mult)
    project((2, 3))
    hs = _linear_scan(a, mult * inp * u, h_ref[...])
    h_ref[...] = jnp.where(i > 0, hs[TT - 1:TT, :], 0.0)
    yc = gact_s[...] * hs
    project((4, 5))

    parts = []
    for gi, win in enumerate(POOL_WINDOWS):
        e = extd_ref[:, gi * G_D:(gi + 1) * G_D]
        span = 1
        while span < win:
            e = e + pltpu.roll(e, span, 0)
            span *= 2
        n_avail = jnp.clip(t_glob + 1, 1, win).astype(F32)
        parts.append(e[POOL_HALO:, :] / n_avail - extd_ref[POOL_HALO:POOL_HALO + TT, gi * G_D:(gi + 1) * G_D])
    extd_ref[0:POOL_HALO, :] = extd_ref[TT:TT + POOL_HALO, :]
    yd = _bdot(jnp.concatenate(parts, axis=-1), dw_ref[...]) * ds_ref[...]

    mix = _bdot(jnp.concatenate([yc, yd], axis=-1), wo_ref[...])
    o_ref[...] = xp_ref[...] + mix


def _odd_mix(x, g, prm):
    TT = TT_ODD
    T = x.shape[0]
    n_tiles = T // TT
    cur = pl.BlockSpec((TT, D_MODEL), lambda i: (jnp.minimum(i, n_tiles - 1), 0))
    prv = pl.BlockSpec((TT, D_MODEL), lambda i: (jnp.maximum(i - 1, 0), 0))
    vec = _resident((1, D_C))
    sq = _resident((D_C, D_C))
    return pl.pallas_call(
        _odd_kernel,
        grid=(n_tiles + 1,),
        in_specs=[cur, prv, _resident((1, D_MODEL)), _resident((D_MODEL, 2 * D_C + D_D)),
                  _resident(prm["conv_w"].shape), vec, sq, vec, sq, vec, vec, sq, vec,
                  _resident((D_MODEL, D_MODEL))],
        out_specs=prv,
        out_shape=jax.ShapeDtypeStruct((T, D_MODEL), F32),
        scratch_shapes=[pltpu.VMEM((TT, 2 * D_C), F32), pltpu.VMEM((TT, D_D), F32), pltpu.VMEM((TT, D_C), F32),
                        pltpu.VMEM((HALO + TT, D_C), F32), pltpu.VMEM((POOL_HALO + TT, D_D), F32),
                        pltpu.VMEM((1, D_C), F32)],
        compiler_params=_params(),
        name="odd_mix",
    )(x, x, g, prm["w_in"], prm["conv_w"], prm["conv_b"], prm["wa"], prm["ba"], prm["wx"], prm["bx"], prm["lam"],
      prm["dw"], prm["ds"], prm["w_out"])


def _block_diag(w):
    H, n, m = w.shape
    return jnp.einsum("hij,hg->higj", w, jnp.eye(H, dtype=w.dtype)).reshape(H * n, H * m)


def _pad_rows(w, before, total):
    return jnp.pad(w, ((before, total - before - w.shape[0]), (0, 0)))


def kernel(x, norm_g, ffn_wg, ffn_wu, ffn_wd, even_w_in, even_w_out, a_mu, a_w0, a_w_up, a_a0, a_a_up, a_g_up,
           a_k_k, a_k_a, a_r_k, a_ln_w, a_ln_b, a_v0, a_v_dn, a_v_up, b_conv_w, odd_w_in, odd_w_out, c_conv_w,
           c_conv_b, c_wa, c_ba, c_wx, c_bx, c_lam, d_w, d_scale, final_g):
    B, T, D = x.shape
    assert (B, D) == (1, D_MODEL) and T % TM_FFN == 0 and T % TT == 0 and T % TT_ODD == 0
    xt = x.reshape(T, D)
    ffn_f32 = (ffn_wg, ffn_wu, ffn_wd)
    w_ffn = tuple(w[0, 0].astype(BF16) for w in ffn_f32)
    n_ffn = 2 * DEPTH
    following = lambda j: (*ffn_f32, (j + 1) // 2, (j + 1) % 2) if j + 1 < n_ffn else None
    row = lambda v: v.reshape(1, -1)
    lora_w = a_w_up.shape[1]
    seg = _block_diag(jnp.ones((GROUP_W // HEAD_A, HEAD_A, HEAD_A), BF16))
    v_first = None
    for layer in range(DEPTH):
        xt, w_ffn = _ffn(xt, row(norm_g[layer, 0]), w_ffn, following(2 * layer))
        g_mix = row(norm_g[layer, 1])
        if layer % 2 == 0:
            e = layer // 2
            prm = dict(
                w_in=even_w_in[e].astype(BF16), w_out=even_w_out[e].astype(BF16), conv_w=b_conv_w[e],
                mu=row(a_mu[e]), w0=row(a_w0[e]), a0=row(a_a0[e]),
                w_up=_pad_rows(a_w_up[e], 0, LANES).astype(BF16),
                a_up=_pad_rows(a_a_up[e], lora_w, LANES).astype(BF16),
                g_up=a_g_up[e].astype(BF16),
                k_k=row(a_k_k[e]), k_a=row(a_k_a[e]), r_k=row(a_r_k[e]), ln_w=row(a_ln_w[e]), ln_b=row(a_ln_b[e]),
                seg=seg)
            vres = None
            if e > 0:
                vres = dict(v0=row(a_v0[e - 1]),
                            v_dn=jnp.pad(a_v_dn[e - 1], ((0, 0), (0, LANES - LORA_V))).astype(BF16),
                            v_up=_pad_rows(a_v_up[e - 1], 0, LANES).astype(BF16))
            xt, v_first = _even_mix(xt, g_mix, prm, v_first, vres)
        else:
            o = layer // 2
            prm = dict(
                w_in=odd_w_in[o].astype(BF16), w_out=odd_w_out[o].astype(BF16),
                conv_w=c_conv_w[o], conv_b=row(c_conv_b[o]),
                wa=_block_diag(c_wa[o]).astype(BF16), ba=row(c_ba[o]),
                wx=_block_diag(c_wx[o]).astype(BF16), bx=row(c_bx[o]), lam=row(c_lam[o]),
                dw=_block_diag(d_w[o]).astype(BF16), ds=row(d_scale[o]))
            xt = _odd_mix(xt, g_mix, prm)
        last = layer == DEPTH - 1
        xt, w_ffn = _ffn(xt, row(norm_g[layer, 2]), w_ffn, following(2 * layer + 1),
                         final_g=row(final_g) if last else None)
    return xt.reshape(B, T, D)
```
